```python
import math
import jax, jax.numpy as jnp
from jax import lax
import numpy as np


D_MODEL = 1024
BATCH = 32
SEQ = 2048
DEPTH = 1

CHUNK = 64
RET_HEADS = 4
RET_HEAD_DIM = 128
RET_WIDTH = RET_HEADS * RET_HEAD_DIM
ROPE_BASE = 10000.0
SG_BLOCK = 128
SG_GROUPS = 4
SG_DIM = 128
SG_WIDTH = SG_GROUPS * SG_DIM
MIX_WIDTH = RET_WIDTH + SG_WIDTH
IN_SPLITS = (RET_WIDTH, RET_WIDTH, RET_WIDTH, RET_WIDTH, SG_WIDTH, SG_WIDTH)
IN_WIDTH = sum(IN_SPLITS)
D_FF = 4 * D_MODEL
PLE_DIM = 256
EPS = 1e-6

kernel_name = "hybrid_retention_gmlp_streaming_layer"


def rmsnorm(x, g):
    xf = x.astype(jnp.float32)
    y = xf * lax.rsqrt(jnp.mean(xf * xf, axis=-1, keepdims=True) + EPS)
    return (y * g.astype(jnp.float32)).astype(x.dtype)


def layernorm_nogain(xf):
    mu = jnp.mean(xf, axis=-1, keepdims=True)
    xc = xf - mu
    return xc * lax.rsqrt(jnp.mean(xc * xc, axis=-1, keepdims=True) + EPS)


def rotary(x, pos):
    d = x.shape[-1]
    freqs = ROPE_BASE ** (-jnp.arange(0, d, 2, dtype=jnp.float32) / d)
    ang = pos[:, None] * freqs[None, :]
    cos = jnp.cos(ang)[None, :, None, :].astype(x.dtype)
    sin = jnp.sin(ang)[None, :, None, :].astype(x.dtype)
    x1, x2 = x[..., : d // 2], x[..., d // 2:]
    return jnp.concatenate([x1 * cos - x2 * sin, x2 * cos + x1 * sin], axis=-1)


def chunk_retention(q, k, v):
    B, S, H, D = q.shape
    N = S // CHUNK
    q, k, v = (a.astype(jnp.float32) for a in (q, k, v))
    log_g = jnp.log(1.0 - 2.0 ** (-5.0 - jnp.arange(H, dtype=jnp.float32)))
    t = jnp.arange(CHUNK, dtype=jnp.float32)
    intra_decay = jnp.exp(jnp.abs(t[:, None] - t[None, :])[None] * log_g[:, None, None])
    q_dec = jnp.exp((t[:, None] + 1.0) * log_g[None, :])
    k_dec = jnp.exp((CHUNK - 1.0 - t)[:, None] * log_g[None, :])
    c_dec = jnp.exp(CHUNK * log_g)

    qc = q.reshape(B, N, CHUNK, H, D)
    kc = k.reshape(B, N, CHUNK, H, D)
    vc = v.reshape(B, N, CHUNK, H, D)
    scores = jnp.einsum('bnihd,bnjhd->bnhij', qc, kc) * intra_decay
    o_intra = jnp.einsum('bnhij,bnjhe->bnihe', scores, vc)

    def step(state, inp):
        qn, kn, vn = inp
        o = jnp.einsum('bihd,bhde->bihe', qn * q_dec[None, :, :, None], state)
        state = state * c_dec[None, :, None, None] + jnp.einsum(
            'bjhd,bjhe->bhde', kn * k_dec[None, :, :, None], vn)
        return state, o

    xs = (qc.transpose(1, 0, 2, 3, 4), kc.transpose(1, 0, 2, 3, 4), vc.transpose(1, 0, 2, 3, 4))
    state0 = jnp.zeros((B, H, D, D), jnp.float32)
    _, o_cross = lax.scan(step, state0, xs)
    o = o_intra + o_cross.transpose(1, 0, 2, 3, 4)
    return o.reshape(B, S, H, D)


def spatial_gating(u, sv, w_s, b_s, sg_g):
    B, S, _ = sv.shape
    svn = (layernorm_nogain(sv.astype(jnp.float32)) * sg_g.astype(jnp.float32)).astype(sv.dtype)
    vb = svn.reshape(B, S // SG_BLOCK, SG_BLOCK, SG_GROUPS, SG_DIM)
    idx = jnp.arange(SG_BLOCK)
    allowed = (idx[None, :] // CHUNK) <= (idx[:, None] // CHUNK)
    w = jnp.where(allowed[None], w_s, jnp.zeros_like(w_s)).astype(sv.dtype)
    s = jnp.einsum('gij,bnjgc->bnigc', w, vb) + b_s.T.astype(sv.dtype)[None, None, :, :, None]
    return u * s.reshape(B, S, SG_WIDTH)


def setup_inputs(seed: int = 0) -> dict:
    key = jax.random.key(seed)
    ks = jax.random.split(key, 20)
    f32 = jnp.float32
    nrm = lambda k, shape, scale: jax.random.normal(k, shape, f32) * scale
    gain = lambda k, shape: 1.0 + 0.01 * jax.random.normal(k, shape, f32)
    return {
        "x": jax.random.normal(ks[0], (BATCH, SEQ, D_MODEL), f32),
        "p": jax.random.normal(ks[1], (DEPTH, BATCH, SEQ, PLE_DIM), f32),
        "g_mix": gain(ks[2], (DEPTH, D_MODEL)),
        "w_in": nrm(ks[3], (DEPTH, D_MODEL, IN_WIDTH), D_MODEL ** -0.5),
        "ret_norm_g": gain(ks[4], (DEPTH, RET_WIDTH)),
        "sg_norm_g": gain(ks[5], (DEPTH, SG_WIDTH)),
        "w_s": nrm(ks[6], (DEPTH, SG_GROUPS, SG_BLOCK, SG_BLOCK), SG_BLOCK ** -0.5),
        "b_s": gain(ks[7], (DEPTH, SG_GROUPS, SG_BLOCK)),
        "w_out": nrm(ks[8], (DEPTH, MIX_WIDTH, D_MODEL), MIX_WIDTH ** -0.5),
        "g_ffn": gain(ks[9], (DEPTH, D_MODEL)),
        "w_ff1": nrm(ks[10], (DEPTH, D_MODEL, D_FF), D_MODEL ** -0.5),
        "w_ff2": nrm(ks[11], (DEPTH, D_FF, D_MODEL), D_FF ** -0.5),
        "g_ple": gain(ks[12], (DEPTH, D_MODEL)),
        "w_ple_gate": nrm(ks[13], (DEPTH, D_MODEL, D_MODEL), D_MODEL ** -0.5),
        "w_ple": nrm(ks[14], (DEPTH, PLE_DIM, D_MODEL), PLE_DIM ** -0.5),
        "g_final": gain(ks[15], (D_MODEL,)),
    }


def reference(x, p, g_mix, w_in, ret_norm_g, sg_norm_g, w_s, b_s, w_out, g_ffn,
              w_ff1, w_ff2, g_ple, w_ple_gate, w_ple, g_final):
    B, S, _ = x.shape
    pos = jnp.arange(S, dtype=jnp.float32)
    cuts = list(np.cumsum(IN_SPLITS)[:-1])
    for i in range(DEPTH):
        h = rmsnorm(x, g_mix[i])
        z = h @ w_in[i]
        q, k, v, g, u, sv = jnp.split(z, cuts, axis=-1)
        q = rotary(q.reshape(B, S, RET_HEADS, RET_HEAD_DIM), pos)
        k = rotary(k.reshape(B, S, RET_HEADS, RET_HEAD_DIM), pos) * (RET_HEAD_DIM ** -0.5)
        v = v.reshape(B, S, RET_HEADS, RET_HEAD_DIM)
        r = chunk_retention(q, k, v)
        r = layernorm_nogain(r).reshape(B, S, RET_WIDTH) * ret_norm_g[i].astype(jnp.float32)
        y_ret = jax.nn.silu(g) * r.astype(x.dtype)
        y_sg = spatial_gating(jax.nn.gelu(u), jax.nn.gelu(sv), w_s[i], b_s[i], sg_norm_g[i])
        x = x + jnp.concatenate([y_ret, y_sg], axis=-1) @ w_out[i]
        hf = rmsnorm(x, g_ffn[i]) @ w_ff1[i]
        x = x + jnp.square(jax.nn.relu(hf)) @ w_ff2[i]
        gate = jax.nn.sigmoid(rmsnorm(x, g_ple[i]) @ w_ple_gate[i])
        x = x + gate * (p[i] @ w_ple[i])
    return rmsnorm(x, g_final)
```

```python
import functools
import math

import jax
import jax.numpy as jnp
from jax import lax
from jax.experimental import pallas as pl
from jax.experimental.pallas import tpu as pltpu

D_MODEL = 1024
CHUNK = 64
RET_HEADS = 4
HEAD_DIM = 128
RET_WIDTH = RET_HEADS * HEAD_DIM
SG_BLOCK = 128
SG_GROUPS = 4
SG_DIM = 128
SG_WIDTH = SG_GROUPS * SG_DIM
IN_WIDTH = 4 * RET_WIDTH + 2 * SG_WIDTH
D_FF = 4 * D_MODEL
PLE_DIM = 256
ROPE_BASE = 10000.0
EPS = 1e-6

SEQ_TILE = 256
ROW_TILE = 512
V7X_VMEM_LIMIT_BYTES = 56 * 1024 * 1024

_GAMMAS = tuple(1.0 - 2.0 ** (-5.0 - h) for h in range(RET_HEADS))


def _bf16(a):
    return a.astype(jnp.bfloat16)


def _dot(a, b):
    return jnp.dot(a, b, preferred_element_type=jnp.float32)


def _rms_scale(x):
    return lax.rsqrt(jnp.mean(x * x, axis=-1, keepdims=True) + EPS)


def _layernorm_nogain(x):
    mu = jnp.mean(x, axis=-1, keepdims=True)
    xc = x - mu
    return xc * lax.rsqrt(jnp.mean(xc * xc, axis=-1, keepdims=True) + EPS)


def _mixer_kernel(x_ref, g_mix_ref, w_in_ref, cq_ref, sq_ref, ck_ref, sk_ref,
                  dmat_ref, qdec_ref, kdec_ref, ret_g_ref, sg_g_ref, w_s_ref, b_s_ref,
                  w_out_ref, o_ref, state_ref):
    ts = x_ref.shape[1]

    @pl.when(pl.program_id(1) == 0)
    def _():
        state_ref[...] = jnp.zeros_like(state_ref)

    x = x_ref[0]
    h = _bf16(x * _rms_scale(x) * g_mix_ref[...])
    z = _dot(h, w_in_ref[...])

    def cols(part, idx, width=HEAD_DIM):
        lo = part * RET_WIDTH + idx * width
        return z[:, lo:lo + width]

    cq, sq, ck, sk = cq_ref[...], sq_ref[...], ck_ref[...], sk_ref[...]
    y_parts = []
    for hd in range(RET_HEADS):
        qh, kh, vh, gh = cols(0, hd), cols(1, hd), cols(2, hd), cols(3, hd)
        qr = qh * cq + pltpu.roll(qh, HEAD_DIM // 2, 1) * sq
        kr = kh * ck + pltpu.roll(kh, HEAD_DIM // 2, 1) * sk
        vb = _bf16(vh)
        scores = lax.dot_general(_bf16(qr), _bf16(kr), (((1,), (1,)), ((), ())),
                                 preferred_element_type=jnp.float32)
        o = _dot(_bf16(scores * dmat_ref[hd]), vb)
        o = o + _dot(_bf16(qr * qdec_ref[hd]), _bf16(state_ref[hd]))
        kv = lax.dot_general(_bf16(kr * kdec_ref[hd]), vb, (((0,), (0,)), ((), ())),
                             preferred_element_type=jnp.float32)
        state_ref[hd] = state_ref[hd] * (_GAMMAS[hd] ** ts) + kv
        r = _layernorm_nogain(o) * ret_g_ref[:, hd * HEAD_DIM:(hd + 1) * HEAD_DIM]
        y_parts.append(jax.nn.silu(gh) * r)

    u = jax.nn.gelu(z[:, 4 * RET_WIDTH:4 * RET_WIDTH + SG_WIDTH])
    sv = jax.nn.gelu(z[:, 4 * RET_WIDTH + SG_WIDTH:])
    svn = _bf16(_layernorm_nogain(sv) * sg_g_ref[...])
    row = lax.broadcasted_iota(jnp.int32, (SG_BLOCK, SG_BLOCK), 0)
    col = lax.broadcasted_iota(jnp.int32, (SG_BLOCK, SG_BLOCK), 1)
    allowed = (col // CHUNK) <= (row // CHUNK)
    n_blk = ts // SG_BLOCK
    for g in range(SG_GROUPS):
        w_g = _bf16(jnp.where(allowed, w_s_ref[g], 0.0))
        sl = slice(g * SG_DIM, (g + 1) * SG_DIM)
        vcat = jnp.concatenate(
            [svn[b * SG_BLOCK:(b + 1) * SG_BLOCK, sl] for b in range(n_blk)], axis=1)
        s_cat = _dot(w_g, vcat) + b_s_ref[:, g:g + 1]
        s_g = jnp.concatenate(
            [s_cat[:, b * SG_DIM:(b + 1) * SG_DIM] for b in range(n_blk)], axis=0)
        y_parts.append(u[:, sl] * s_g)

    y = _bf16(jnp.concatenate(y_parts, axis=1))
    o_ref[0] = x + _dot(y, w_out_ref[...])


def _ffn_kernel(x_ref, p_ref, g_ffn_ref, w1_ref, w2_ref, g_ple_ref, w_gate_ref, w_ple_ref,
                g_final_ref, o_ref):
    x = x_ref[...]
    hn = _bf16(x * _rms_scale(x) * g_ffn_ref[...])
    hf = jnp.maximum(_dot(hn, w1_ref[...]), 0.0)
    x = x + _dot(_bf16(hf * hf), w2_ref[...])
    gn = _bf16(x * _rms_scale(x) * g_ple_ref[...])
    gate = jax.nn.sigmoid(_dot(gn, w_gate_ref[...]))
    x = x + gate * _dot(_bf16(p_ref[...]), w_ple_ref[...])
    o_ref[...] = x * _rms_scale(x) * g_final_ref[...]


def _const_spec(shape):
    zeros = (0,) * len(shape)
    return pl.BlockSpec(shape, lambda *_: zeros, pipeline_mode=pl.Buffered(1))


def _retention_tables(seq, ts):
    f32 = jnp.float32
    pos = jnp.arange(seq, dtype=f32)
    freqs = ROPE_BASE ** (-jnp.arange(0, HEAD_DIM, 2, dtype=f32) / HEAD_DIM)
    ang = pos[:, None] * freqs[None, :]
    cos, sin = jnp.cos(ang), jnp.sin(ang)
    cq = jnp.concatenate([cos, cos], axis=1)
    sq = jnp.concatenate([-sin, sin], axis=1)
    k_scale = HEAD_DIM ** -0.5
    log_g = jnp.log(jnp.asarray(_GAMMAS, f32))
    t = jnp.arange(ts, dtype=f32)
    dist = jnp.abs(t[:, None] - t[None, :])
    visible = (t[None, :] // CHUNK) <= (t[:, None] // CHUNK)
    dmat = jnp.where(visible[None], jnp.exp(dist[None] * log_g[:, None, None]), 0.0)
    qdec = jnp.exp((t[None, :] + 1.0) * log_g[:, None])
    kdec = jnp.exp((ts - 1.0 - t)[None, :] * log_g[:, None])
    bcast = lambda a: jnp.broadcast_to(a[:, :, None], (RET_HEADS, ts, HEAD_DIM))
    return cq, sq, cq * k_scale, sq * k_scale, dmat, bcast(qdec), bcast(kdec)


def kernel(x, p, g_mix, w_in, ret_norm_g, sg_norm_g, w_s, b_s, w_out, g_ffn, w_ff1, w_ff2,
           g_ple, w_ple_gate, w_ple, g_final):
    batch, seq, d = x.shape
    depth = w_in.shape[0]
    ts, tm = SEQ_TILE, ROW_TILE
    assert d == D_MODEL and seq % ts == 0 and (batch * seq) % tm == 0
    cq, sq, ck, sk, dmat, qdec, kdec = _retention_tables(seq, ts)
    params = pltpu.CompilerParams(
        dimension_semantics=("arbitrary", "arbitrary"),
        vmem_limit_bytes=V7X_VMEM_LIMIT_BYTES)
    tab_spec = pl.BlockSpec((ts, HEAD_DIM), lambda b, s: (s, 0))
    x_spec = pl.BlockSpec((1, ts, d), lambda b, s: (b, s, 0))
    row_spec = lambda w: pl.BlockSpec((tm, w), lambda i: (i, 0))

    for i in range(depth):
        x = pl.pallas_call(
            _mixer_kernel,
            grid=(batch, seq // ts),
            in_specs=[x_spec, _const_spec((1, d)), _const_spec((d, IN_WIDTH)),
                      tab_spec, tab_spec, tab_spec, tab_spec,
                      _const_spec((RET_HEADS, ts, ts)),
                      _const_spec((RET_HEADS, ts, HEAD_DIM)),
                      _const_spec((RET_HEADS, ts, HEAD_DIM)),
                      _const_spec((1, RET_WIDTH)), _const_spec((1, SG_WIDTH)),
                      _const_spec((SG_GROUPS, SG_BLOCK, SG_BLOCK)),
                      _const_spec((SG_BLOCK, SG_GROUPS)),
                      _const_spec((d, d))],
            out_specs=x_spec,
            out_shape=jax.ShapeDtypeStruct((batch, seq, d), jnp.float32),
            scratch_shapes=[pltpu.VMEM((RET_HEADS, HEAD_DIM, HEAD_DIM), jnp.float32)],
            compiler_params=params,
            name="token_mixer",
        )(x, g_mix[i][None], _bf16(w_in[i]), cq, sq, ck, sk, dmat, qdec, kdec,
          ret_norm_g[i][None], sg_norm_g[i][None], w_s[i], b_s[i].T, _bf16(w_out[i]))

        last = i == depth - 1
        g_last = g_final[None] if last else jnp.ones((1, d), jnp.float32)
        assert last, "the final RMSNorm is fused into the channel mixer of the last layer"
        x = pl.pallas_call(
            _ffn_kernel,
            grid=(batch * seq // tm,),
            in_specs=[row_spec(d), row_spec(PLE_DIM), _const_spec((1, d)),
                      _const_spec((d, D_FF)), _const_spec((D_FF, d)), _const_spec((1, d)),
                      _const_spec((d, d)), _const_spec((PLE_DIM, d)), _const_spec((1, d))],
            out_specs=row_spec(d),
            out_shape=jax.ShapeDtypeStruct((batch * seq, d), jnp.float32),
            compiler_params=pltpu.CompilerParams(
                dimension_semantics=("arbitrary",),
                vmem_limit_bytes=V7X_VMEM_LIMIT_BYTES),
            name="channel_mixer",
        )(x.reshape(batch * seq, d), p[i].reshape(batch * seq, PLE_DIM), g_ffn[i][None],
          _bf16(w_ff1[i]), _bf16(w_ff2[i]), g_ple[i][None], _bf16(w_ple_gate[i]),
          _bf16(w_ple[i]), g_last).reshape(batch, seq, d)
    return x
```

```python
import jax
import jax.numpy as jnp
from jax import lax
from jax.experimental import pallas as pl
from jax.experimental.pallas import tpu as pltpu

D_MODEL = 1024
CHUNK = 64
RET_HEADS = 4
HEAD_DIM = 128
RET_WIDTH = RET_HEADS * HEAD_DIM
SG_BLOCK = 128
SG_GROUPS = 4
SG_DIM = 128
SG_WIDTH = SG_GROUPS * SG_DIM
IN_WIDTH = 4 * RET_WIDTH + 2 * SG_WIDTH
D_FF = 4 * D_MODEL
PLE_DIM = 256
ROPE_BASE = 10000.0
EPS = 1e-6

SEQ_TILE = 512
RET_BLOCK = 256
ROW_TILE = 512
V7X_VMEM_LIMIT_BYTES = 56 * 1024 * 1024

_GAMMAS = tuple(1.0 - 2.0 ** (-5.0 - h) for h in range(RET_HEADS))


def _bf16(a):
    return a.astype(jnp.bfloat16)


def _dot(a, b):
    return jnp.dot(a, b, preferred_element_type=jnp.float32)


def _rms_scale(x):
    return lax.rsqrt(jnp.mean(x * x, axis=-1, keepdims=True) + EPS)


def _layernorm_nogain(x):
    mu = jnp.mean(x, axis=-1, keepdims=True)
    xc = x - mu
    return xc * lax.rsqrt(jnp.mean(xc * xc, axis=-1, keepdims=True) + EPS)


def _mixer_kernel(x_ref, g_mix_ref, w_in_ref, cq_ref, sq_ref, ck_ref, sk_ref,
                  dmat_ref, qdec_ref, kdec_ref, ret_g_ref, sg_g_ref, w_s_ref, b_s_ref,
                  w_out_ref, o_ref, state_ref):
    ts = x_ref.shape[1]

    @pl.when(pl.program_id(1) == 0)
    def _():
        state_ref[...] = jnp.zeros_like(state_ref)

    x = x_ref[0]
    h = _bf16(x * _rms_scale(x) * g_mix_ref[...])

    z = _dot(h, w_in_ref[...])

    def cols(part, idx):
        lo = part * RET_WIDTH + idx * HEAD_DIM
        return z[:, lo:lo + HEAD_DIM]

    cq, sq, ck, sk = cq_ref[...], sq_ref[...], ck_ref[...], sk_ref[...]
    y_parts = []
    for hd in range(RET_HEADS):
        qh, kh, vh, gh = cols(0, hd), cols(1, hd), cols(2, hd), cols(3, hd)
        qr = qh * cq + pltpu.roll(qh, HEAD_DIM // 2, 1) * sq
        kr = kh * ck + pltpu.roll(kh, HEAD_DIM // 2, 1) * sk
        vb = _bf16(vh)
        o_blocks = []
        for blk in range(ts // RET_BLOCK):
            rows = slice(blk * RET_BLOCK, (blk + 1) * RET_BLOCK)
            qb, kb, vbb = qr[rows], kr[rows], vb[rows]
            scores = lax.dot_general(_bf16(qb), _bf16(kb), (((1,), (1,)), ((), ())),
                                     preferred_element_type=jnp.float32)
            o = _dot(_bf16(scores * dmat_ref[hd]), vbb)
            o_blocks.append(o + _dot(_bf16(qb * qdec_ref[hd]), _bf16(state_ref[hd])))
            kv = lax.dot_general(_bf16(kb * kdec_ref[hd]), vbb, (((0,), (0,)), ((), ())),
                                 preferred_element_type=jnp.float32)
            state_ref[hd] = state_ref[hd] * (_GAMMAS[hd] ** RET_BLOCK) + kv
        o = jnp.concatenate(o_blocks, axis=0)
        r = _layernorm_nogain(o) * ret_g_ref[:, hd * HEAD_DIM:(hd + 1) * HEAD_DIM]
        y_parts.append(jax.nn.silu(gh) * r)

    u = jax.nn.gelu(z[:, 4 * RET_WIDTH:4 * RET_WIDTH + SG_WIDTH])
    sv = jax.nn.gelu(z[:, 4 * RET_WIDTH + SG_WIDTH:])
    svn = _bf16(_layernorm_nogain(sv) * sg_g_ref[...])
    row = lax.broadcasted_iota(jnp.int32, (SG_BLOCK, SG_BLOCK), 0)
    col = lax.broadcasted_iota(jnp.int32, (SG_BLOCK, SG_BLOCK), 1)
    allowed = (col // CHUNK) <= (row // CHUNK)
    n_blk = ts // SG_BLOCK
    for g in range(SG_GROUPS):
        w_g = _bf16(jnp.where(allowed, w_s_ref[g], 0.0))
        sl = slice(g * SG_DIM, (g + 1) * SG_DIM)
        vcat = jnp.concatenate(
            [svn[b * SG_BLOCK:(b + 1) * SG_BLOCK, sl] for b in range(n_blk)], axis=1)
        s_cat = _dot(w_g, vcat) + b_s_ref[:, g:g + 1]
        s_g = jnp.concatenate(
            [s_cat[:, b * SG_DIM:(b + 1) * SG_DIM] for b in range(n_blk)], axis=0)
        y_parts.append(u[:, sl] * s_g)

    y = _bf16(jnp.concatenate(y_parts, axis=1))
    o_ref[0] = x + _dot(y, w_out_ref[...])


def _ffn_kernel(x_ref, p_ref, g_ffn_ref, w1_ref, w2_ref, g_ple_ref, w_gate_ref, w_ple_ref,
                g_final_ref, o_ref):
    x = x_ref[...]
    hn = _bf16(x * _rms_scale(x) * g_ffn_ref[...])
    hf = jnp.maximum(_dot(hn, w1_ref[...]), 0.0)
    x = x + _dot(_bf16(hf * hf), w2_ref[...])
    gn = _bf16(x * _rms_scale(x) * g_ple_ref[...])
    gate = jax.nn.sigmoid(_dot(gn, w_gate_ref[...]))
    x = x + gate * _dot(_bf16(p_ref[...]), w_ple_ref[...])
    o_ref[...] = x * _rms_scale(x) * g_final_ref[...]


def _const_spec(shape):
    zeros = (0,) * len(shape)
    return pl.BlockSpec(shape, lambda *_: zeros, pipeline_mode=pl.Buffered(1))


def _retention_tables(seq, rb):
    f32 = jnp.float32
    pos = jnp.arange(seq, dtype=f32)
    freqs = ROPE_BASE ** (-jnp.arange(0, HEAD_DIM, 2, dtype=f32) / HEAD_DIM)
    ang = pos[:, None] * freqs[None, :]
    cos, sin = jnp.cos(ang), jnp.sin(ang)
    cq = jnp.concatenate([cos, cos], axis=1)
    sq = jnp.concatenate([-sin, sin], axis=1)
    k_scale = HEAD_DIM ** -0.5
    log_g = jnp.log(jnp.asarray(_GAMMAS, f32))
    t = jnp.arange(rb, dtype=f32)
    dist = jnp.abs(t[:, None] - t[None, :])
    visible = (t[None, :] // CHUNK) <= (t[:, None] // CHUNK)
    dmat = jnp.where(visible[None], jnp.exp(dist[None] * log_g[:, None, None]), 0.0)
    qdec = jnp.exp((t[None, :] + 1.0) * log_g[:, None])
    kdec = jnp.exp((rb - 1.0 - t)[None, :] * log_g[:, None])
    bcast = lambda a: jnp.broadcast_to(a[:, :, None], (RET_HEADS, rb, HEAD_DIM))
    return cq, sq, cq * k_scale, sq * k_scale, dmat, bcast(qdec), bcast(kdec)


def kernel(x, p, g_mix, w_in, ret_norm_g, sg_norm_g, w_s, b_s, w_out, g_ffn, w_ff1, w_ff2,
           g_ple, w_ple_gate, w_ple, g_final):
    batch, seq, d = x.shape
    ts, tm, rb = SEQ_TILE, ROW_TILE, RET_BLOCK
    assert w_in.shape[0] == 1, "one layer: the final RMSNorm is fused into its channel mixer"
    assert d == D_MODEL and seq % ts == 0 and (batch * seq) % tm == 0
    cq, sq, ck, sk, dmat, qdec, kdec = _retention_tables(seq, rb)
    tab_spec = pl.BlockSpec((ts, HEAD_DIM), lambda b, s: (s, 0))
    x_spec = pl.BlockSpec((1, ts, d), lambda b, s: (b, s, 0))
    row_spec = lambda w: pl.BlockSpec((tm, w), lambda i: (i, 0))

    x = pl.pallas_call(
        _mixer_kernel,
        grid=(batch, seq // ts),
        in_specs=[x_spec, _const_spec((1, d)), _const_spec((d, IN_WIDTH)),
                  tab_spec, tab_spec, tab_spec, tab_spec,
                  _const_spec((RET_HEADS, rb, rb)),
                  _const_spec((RET_HEADS, rb, HEAD_DIM)),
                  _const_spec((RET_HEADS, rb, HEAD_DIM)),
                  _const_spec((1, RET_WIDTH)), _const_spec((1, SG_WIDTH)),
                  _const_spec((SG_GROUPS, SG_BLOCK, SG_BLOCK)),
                  _const_spec((SG_BLOCK, SG_GROUPS)),
                  _const_spec((d, d))],
        out_specs=x_spec,
        out_shape=jax.ShapeDtypeStruct((batch, seq, d), jnp.float32),
        scratch_shapes=[pltpu.VMEM((RET_HEADS, HEAD_DIM, HEAD_DIM), jnp.float32)],
        compiler_params=pltpu.CompilerParams(
            dimension_semantics=("arbitrary", "arbitrary"),
            vmem_limit_bytes=V7X_VMEM_LIMIT_BYTES),
        name="token_mixer",
    )(x, g_mix[0][None], _bf16(w_in[0]), cq, sq, ck, sk, dmat, qdec, kdec,
      ret_norm_g[0][None], sg_norm_g[0][None], w_s[0], b_s[0].T, _bf16(w_out[0]))

    x = pl.pallas_call(
        _ffn_kernel,
        grid=(batch * seq // tm,),
        in_specs=[row_spec(d), row_spec(PLE_DIM), _const_spec((1, d)),
                  _const_spec((d, D_FF)), _const_spec((D_FF, d)), _const_spec((1, d)),
                  _const_spec((d, d)), _const_spec((PLE_DIM, d)), _const_spec((1, d))],
        out_specs=row_spec(d),
        out_shape=jax.ShapeDtypeStruct((batch * seq, d), jnp.float32),
        compiler_params=pltpu.CompilerParams(
            dimension_semantics=("arbitrary",),
            vmem_limit_bytes=V7X_VMEM_LIMIT_BYTES),
        name="channel_mixer",
    )(x.reshape(batch * seq, d), p[0].reshape(batch * seq, PLE_DIM), g_ffn[0][None],
      _bf16(w_ff1[0]), _bf16(w_ff2[0]), g_ple[0][None], _bf16(w_ple_gate[0]),
      _bf16(w_ple[0]), g_final[None])
    return x.reshape(batch, seq, d)
```

```python
import functools

import jax
import jax.numpy as jnp
from jax import lax
from jax.experimental import pallas as pl
from jax.experimental.pallas import tpu as pltpu

D_MODEL = 1024
CHUNK = 64
RET_HEADS = 4
HEAD_DIM = 128
RET_WIDTH = RET_HEADS * HEAD_DIM
SG_BLOCK = 128
SG_GROUPS = 4
SG_DIM = 128
SG_WIDTH = SG_GROUPS * SG_DIM
IN_WIDTH = 4 * RET_WIDTH + 2 * SG_WIDTH
D_FF = 4 * D_MODEL
PLE_DIM = 256
ROPE_BASE = 10000.0
EPS = 1e-6

SEQ_TILE = 512
RET_BLOCK = 256
ROW_TILE = 512
PROJ_PARTS = 6
V7X_VMEM_LIMIT_BYTES = 56 * 1024 * 1024

_GAMMAS = tuple(1.0 - 2.0 ** (-5.0 - h) for h in range(RET_HEADS))


def _bf16(a):
    return a.astype(jnp.bfloat16)


def _dot(a, b):
    return jnp.dot(a, b, preferred_element_type=jnp.float32)


def _rms_scale(x):
    return lax.rsqrt(jnp.mean(x * x, axis=-1, keepdims=True) + EPS)


def _layernorm_nogain(x):
    mu = jnp.mean(x, axis=-1, keepdims=True)
    xc = x - mu
    return xc * lax.rsqrt(jnp.mean(xc * xc, axis=-1, keepdims=True) + EPS)


def _project_parts(x_ref, g_mix_ref, w_in_ref, z_ref, xkeep_ref):
    x = x_ref[0]
    xkeep_ref[...] = x
    h = _bf16(x * _rms_scale(x) * g_mix_ref[...])
    width = IN_WIDTH // PROJ_PARTS

    def part(c):
        cs = slice(c * width, (c + 1) * width)
        z_ref[:, cs] = _dot(h, w_in_ref[:, cs])

    return [functools.partial(part, c) for c in range(PROJ_PARTS)]


def _mix_parts(z_ref, x_ref, rows, cq_ref, sq_ref, ck_ref, sk_ref, dmat_ref, qdec_ref, kdec_ref,
               ret_g_ref, sg_g_ref, w_s_ref, b_s_ref, w_out_ref, state_ref, o_ref):
    ts = z_ref.shape[0]
    y_parts = []

    def cols(part, idx):
        lo = part * RET_WIDTH + idx * HEAD_DIM
        return z_ref[:, lo:lo + HEAD_DIM]

    def head(hd):
        cq, sq, ck, sk = cq_ref[rows], sq_ref[rows], ck_ref[rows], sk_ref[rows]
        qh, kh, vh, gh = cols(0, hd), cols(1, hd), cols(2, hd), cols(3, hd)
        qr = qh * cq + pltpu.roll(qh, HEAD_DIM // 2, 1) * sq
        kr = kh * ck + pltpu.roll(kh, HEAD_DIM // 2, 1) * sk
        vb = _bf16(vh)
        o_blocks = []
        for blk in range(ts // RET_BLOCK):
            brows = slice(blk * RET_BLOCK, (blk + 1) * RET_BLOCK)
            qb, kb, vbb = qr[brows], kr[brows], vb[brows]
            kd_t = _bf16(kb * kdec_ref[hd]).T
            scores = lax.dot_general(_bf16(qb), _bf16(kb), (((1,), (1,)), ((), ())),
                                     preferred_element_type=jnp.float32)
            o = _dot(_bf16(scores * dmat_ref[hd]), vbb)
            o_blocks.append(o + _dot(_bf16(qb * qdec_ref[hd]), _bf16(state_ref[hd])))
            state_ref[hd] = state_ref[hd] * (_GAMMAS[hd] ** RET_BLOCK) + _dot(kd_t, vbb)
        o = jnp.concatenate(o_blocks, axis=0)
        r = _layernorm_nogain(o) * ret_g_ref[:, hd * HEAD_DIM:(hd + 1) * HEAD_DIM]
        y_parts.append(jax.nn.silu(gh) * r)

    def gating():
        u = jax.nn.gelu(z_ref[:, 4 * RET_WIDTH:4 * RET_WIDTH + SG_WIDTH])
        sv = jax.nn.gelu(z_ref[:, 4 * RET_WIDTH + SG_WIDTH:])
        svn = _bf16(_layernorm_nogain(sv) * sg_g_ref[...])
        row = lax.broadcasted_iota(jnp.int32, (SG_BLOCK, SG_BLOCK), 0)
        col = lax.broadcasted_iota(jnp.int32, (SG_BLOCK, SG_BLOCK), 1)
        allowed = (col // CHUNK) <= (row // CHUNK)
        n_blk = ts // SG_BLOCK
        for g in range(SG_GROUPS):
            w_g = _bf16(jnp.where(allowed, w_s_ref[g], 0.0))
            sl = slice(g * SG_DIM, (g + 1) * SG_DIM)
            vcat = jnp.concatenate(
                [svn[b * SG_BLOCK:(b + 1) * SG_BLOCK, sl] for b in range(n_blk)], axis=1)
            s_cat = _dot(w_g, vcat) + b_s_ref[:, g:g + 1]
            s_g = jnp.concatenate(
                [s_cat[:, b * SG_DIM:(b + 1) * SG_DIM] for b in range(n_blk)], axis=0)
            y_parts.append(u[:, sl] * s_g)

    def out_proj():
        y = _bf16(jnp.concatenate(y_parts, axis=1))
        o_ref[0, rows, :] = x_ref[...] + _dot(y, w_out_ref[...])

    return [functools.partial(head, hd) for hd in range(RET_HEADS)] + [gating, out_proj]


def _mixer_kernel(x_first_ref, x_odd_ref, x_even_ref, g_mix_ref, w_in_ref,
                  cq_ref, sq_ref, ck_ref, sk_ref, dmat_ref, qdec_ref, kdec_ref,
                  ret_g_ref, sg_g_ref, w_s_ref, b_s_ref, w_out_ref, o_ref,
                  state_ref, z_even_ref, z_odd_ref, xk_even_ref, xk_odd_ref, *, pairs_per_row):
    ts = x_odd_ref.shape[1]
    k = pl.program_id(0)
    project = lambda x_ref, z_ref, xk_ref: _project_parts(
        x_ref, g_mix_ref, w_in_ref, z_ref, xk_ref)
    mix = lambda z_ref, xk_ref, rows: _mix_parts(
        z_ref, xk_ref, rows, cq_ref, sq_ref, ck_ref, sk_ref, dmat_ref, qdec_ref, kdec_ref,
        ret_g_ref, sg_g_ref, w_s_ref, b_s_ref, w_out_ref, state_ref, o_ref)

    def interleave(proj_parts, mix_parts):
        for proj_part, mix_part in zip(proj_parts, mix_parts, strict=True):
            proj_part()
            mix_part()

    @pl.when(k == 0)
    def _():
        for proj_part in project(x_first_ref, z_even_ref, xk_even_ref):
            proj_part()

    @pl.when(k % pairs_per_row == 0)
    def _():
        state_ref[...] = jnp.zeros_like(state_ref)

    interleave(project(x_odd_ref, z_odd_ref, xk_odd_ref),
               mix(z_even_ref, xk_even_ref, slice(0, ts)))
    interleave(project(x_even_ref, z_even_ref, xk_even_ref),
               mix(z_odd_ref, xk_odd_ref, slice(ts, 2 * ts)))


def _ffn_kernel(x_ref, p_ref, g_ffn_ref, w1_ref, w2_ref, g_ple_ref, w_gate_ref, w_ple_ref,
                g_final_ref, o_ref):
    x = x_ref[...]
    hn = _bf16(x * _rms_scale(x) * g_ffn_ref[...])
    hf = jnp.maximum(_dot(hn, w1_ref[...]), 0.0)
    x = x + _dot(_bf16(hf * hf), w2_ref[...])
    gn = _bf16(x * _rms_scale(x) * g_ple_ref[...])
    gate = jax.nn.sigmoid(_dot(gn, w_gate_ref[...]))
    x = x + gate * _dot(_bf16(p_ref[...]), w_ple_ref[...])
    o_ref[...] = x * _rms_scale(x) * g_final_ref[...]


def _const_spec(shape):
    zeros = (0,) * len(shape)
    return pl.BlockSpec(shape, lambda *_: zeros, pipeline_mode=pl.Buffered(1))


def _retention_tables(seq, rb):
    f32 = jnp.float32
    pos = jnp.arange(seq, dtype=f32)
    freqs = ROPE_BASE ** (-jnp.arange(0, HEAD_DIM, 2, dtype=f32) / HEAD_DIM)
    ang = pos[:, None] * freqs[None, :]
    cos, sin = jnp.cos(ang), jnp.sin(ang)
    cq = jnp.concatenate([cos, cos], axis=1)
    sq = jnp.concatenate([-sin, sin], axis=1)
    k_scale = HEAD_DIM ** -0.5
    log_g = jnp.log(jnp.asarray(_GAMMAS, f32))
    t = jnp.arange(rb, dtype=f32)
    dist = jnp.abs(t[:, None] - t[None, :])
    visible = (t[None, :] // CHUNK) <= (t[:, None] // CHUNK)
    dmat = jnp.where(visible[None], jnp.exp(dist[None] * log_g[:, None, None]), 0.0)
    qdec = jnp.exp((t[None, :] + 1.0) * log_g[:, None])
    kdec = jnp.exp((rb - 1.0 - t)[None, :] * log_g[:, None])
    bcast = lambda a: jnp.broadcast_to(a[:, :, None], (RET_HEADS, rb, HEAD_DIM))
    return cq, sq, cq * k_scale, sq * k_scale, dmat, bcast(qdec), bcast(kdec)


def kernel(x, p, g_mix, w_in, ret_norm_g, sg_norm_g, w_s, b_s, w_out, g_ffn, w_ff1, w_ff2,
           g_ple, w_ple_gate, w_ple, g_final):
    batch, seq, d = x.shape
    ts, tm, rb = SEQ_TILE, ROW_TILE, RET_BLOCK
    assert w_in.shape[0] == 1, "one layer: the final RMSNorm is fused into its channel mixer"
    assert d == D_MODEL and seq % ts == 0 and (batch * seq) % tm == 0
    cq, sq, ck, sk, dmat, qdec, kdec = _retention_tables(seq, rb)
    tiles_per_row = seq // ts
    assert tiles_per_row % 2 == 0, "a grid step mixes a pair of seq tiles of one batch row"
    pairs_per_row = tiles_per_row // 2
    n_tiles = batch * tiles_per_row

    def x_tile_spec(tile_of_step):
        def index(k):
            t = jnp.minimum(tile_of_step(k), n_tiles - 1)
            return (t // tiles_per_row, t % tiles_per_row, 0)
        return pl.BlockSpec((1, ts, d), index)

    tab_spec = pl.BlockSpec((2 * ts, HEAD_DIM), lambda k: (k % pairs_per_row, 0))
    pair_spec = pl.BlockSpec((1, 2 * ts, d), lambda k: (k // pairs_per_row, k % pairs_per_row, 0))
    row_spec = lambda w: pl.BlockSpec((tm, w), lambda i: (i, 0))
    f32_tile = lambda w: pltpu.VMEM((ts, w), jnp.float32)

    x = pl.pallas_call(
        functools.partial(_mixer_kernel, pairs_per_row=pairs_per_row),
        grid=(n_tiles // 2,),
        in_specs=[pl.BlockSpec((1, ts, d), lambda k: (0, 0, 0), pipeline_mode=pl.Buffered(1)),
                  x_tile_spec(lambda k: 2 * k + 1), x_tile_spec(lambda k: 2 * k + 2),
                  _const_spec((1, d)), _const_spec((d, IN_WIDTH)),
                  tab_spec, tab_spec, tab_spec, tab_spec,
                  _const_spec((RET_HEADS, rb, rb)),
                  _const_spec((RET_HEADS, rb, HEAD_DIM)),
                  _const_spec((RET_HEADS, rb, HEAD_DIM)),
                  _const_spec((1, RET_WIDTH)), _const_spec((1, SG_WIDTH)),
                  _const_spec((SG_GROUPS, SG_BLOCK, SG_BLOCK)),
                  _const_spec((SG_BLOCK, SG_GROUPS)),
                  _const_spec((d, d))],
        out_specs=pair_spec,
        out_shape=jax.ShapeDtypeStruct((batch, seq, d), jnp.float32),
        scratch_shapes=[pltpu.VMEM((RET_HEADS, HEAD_DIM, HEAD_DIM), jnp.float32),
                        f32_tile(IN_WIDTH), f32_tile(IN_WIDTH), f32_tile(d), f32_tile(d)],
        compiler_params=pltpu.CompilerParams(
            dimension_semantics=("arbitrary",),
            vmem_limit_bytes=V7X_VMEM_LIMIT_BYTES),
        name="token_mixer",
    )(x, x, x, g_mix[0][None], _bf16(w_in[0]), cq, sq, ck, sk, dmat, qdec, kdec,
      ret_norm_g[0][None], sg_norm_g[0][None], w_s[0], b_s[0].T, _bf16(w_out[0]))

    x = pl.pallas_call(
        _ffn_kernel,
        grid=(batch * seq // tm,),
        in_specs=[row_spec(d), row_spec(PLE_DIM), _const_spec((1, d)),
                  _const_spec((d, D_FF)), _const_spec((D_FF, d)), _const_spec((1, d)),
                  _const_spec((d, d)), _const_spec((PLE_DIM, d)), _const_spec((1, d))],
        out_specs=row_spec(d),
        out_shape=jax.ShapeDtypeStruct((batch * seq, d), jnp.float32),
        compiler_params=pltpu.CompilerParams(
            dimension_semantics=("arbitrary",),
            vmem_limit_bytes=V7X_VMEM_LIMIT_BYTES),
        name="channel_mixer",
    )(x.reshape(batch * seq, d), p[0].reshape(batch * seq, PLE_DIM), g_ffn[0][None],
      _bf16(w_ff1[0]), _bf16(w_ff2[0]), g_ple[0][None], _bf16(w_ple_gate[0]),
      _bf16(w_ple[0]), g_final[None])
    return x.reshape(batch, seq, d)
```

```python
import functools

import jax
import jax.numpy as jnp
from jax import lax
from jax.experimental import pallas as pl
from jax.experimental.pallas import tpu as pltpu

D_MODEL = 1024
CHUNK = 64
RET_HEADS = 4
HEAD_DIM = 128
RET_WIDTH = RET_HEADS * HEAD_DIM
SG_BLOCK = 128
SG_GROUPS = 4
SG_DIM = 128
SG_WIDTH = SG_GROUPS * SG_DIM
IN_WIDTH = 4 * RET_WIDTH + 2 * SG_WIDTH
D_FF = 4 * D_MODEL
PLE_DIM = 256
ROPE_BASE = 10000.0
EPS = 1e-6

SEQ_TILE = 512
RET_BLOCK = 256
ROW_TILE = 1024
FFN_SUB_ROWS = 256
PROJ_PARTS = 6
V7X_VMEM_LIMIT_BYTES = 56 * 1024 * 1024

_GAMMAS = tuple(1.0 - 2.0 ** (-5.0 - h) for h in range(RET_HEADS))


def _bf16(a):
    return a.astype(jnp.bfloat16)


def _dot(a, b):
    return jnp.dot(a, b, preferred_element_type=jnp.float32)


def _rms_scale(x):
    return lax.rsqrt(jnp.mean(x * x, axis=-1, keepdims=True) + EPS)


def _layernorm_nogain(x):
    mu = jnp.mean(x, axis=-1, keepdims=True)
    xc = x - mu
    return xc * lax.rsqrt(jnp.mean(xc * xc, axis=-1, keepdims=True) + EPS)


def _project_parts(x_ref, g_mix_ref, w_in_ref, z_ref, xkeep_ref):
    x = x_ref[0]
    xkeep_ref[...] = x
    h = _bf16(x * _rms_scale(x) * g_mix_ref[...])
    width = IN_WIDTH // PROJ_PARTS

    def part(c):
        cs = slice(c * width, (c + 1) * width)
        z_ref[:, cs] = _dot(h, w_in_ref[:, cs])

    return [functools.partial(part, c) for c in range(PROJ_PARTS)]


def _mix_parts(z_ref, x_ref, rows, pos_rows, cos_ref, sin_ref, dmat_ref, qdec_ref, kdec_ref,
               ret_g_ref, sg_g_ref, w_s_ref, b_s_ref, w_out_ref, state_ref, o_ref):
    ts = z_ref.shape[0]
    y_parts = []

    def cols(part, idx):
        lo = part * RET_WIDTH + idx * HEAD_DIM
        return z_ref[:, lo:lo + HEAD_DIM]

    def head(hd):
        cos, sin = cos_ref[pos_rows, :], sin_ref[pos_rows, :]
        qh, kh, vh, gh = cols(0, hd), cols(1, hd), cols(2, hd), cols(3, hd)
        qr = qh * cos + pltpu.roll(qh, HEAD_DIM // 2, 1) * sin
        kr = (kh * cos + pltpu.roll(kh, HEAD_DIM // 2, 1) * sin) * HEAD_DIM ** -0.5
        vb = _bf16(vh)
        o_blocks = []
        for blk in range(ts // RET_BLOCK):
            brows = slice(blk * RET_BLOCK, (blk + 1) * RET_BLOCK)
            qb, kb, vbb = qr[brows], kr[brows], vb[brows]
            kd_t = _bf16(kb * kdec_ref[hd]).T
            scores = lax.dot_general(_bf16(qb), _bf16(kb), (((1,), (1,)), ((), ())),
                                     preferred_element_type=jnp.float32)
            o = _dot(_bf16(scores * dmat_ref[hd]), vbb)
            o_blocks.append(o + _dot(_bf16(qb * qdec_ref[hd]), _bf16(state_ref[hd])))
            state_ref[hd] = state_ref[hd] * (_GAMMAS[hd] ** RET_BLOCK) + _dot(kd_t, vbb)
        o = jnp.concatenate(o_blocks, axis=0)
        r = _layernorm_nogain(o) * ret_g_ref[:, hd * HEAD_DIM:(hd + 1) * HEAD_DIM]
        y_parts.append(jax.nn.silu(gh) * r)

    def gating():
        u = jax.nn.gelu(z_ref[:, 4 * RET_WIDTH:4 * RET_WIDTH + SG_WIDTH])
        sv = jax.nn.gelu(z_ref[:, 4 * RET_WIDTH + SG_WIDTH:])
        svn = _bf16(_layernorm_nogain(sv) * sg_g_ref[...])
        row = lax.broadcasted_iota(jnp.int32, (SG_BLOCK, SG_BLOCK), 0)
        col = lax.broadcasted_iota(jnp.int32, (SG_BLOCK, SG_BLOCK), 1)
        allowed = (col // CHUNK) <= (row // CHUNK)
        n_blk = ts // SG_BLOCK
        for g in range(SG_GROUPS):
            w_g = _bf16(jnp.where(allowed, w_s_ref[g], 0.0))
            sl = slice(g * SG_DIM, (g + 1) * SG_DIM)
            vcat = jnp.concatenate(
                [svn[b * SG_BLOCK:(b + 1) * SG_BLOCK, sl] for b in range(n_blk)], axis=1)
            s_cat = _dot(w_g, vcat) + b_s_ref[:, g:g + 1]
            s_g = jnp.concatenate(
                [s_cat[:, b * SG_DIM:(b + 1) * SG_DIM] for b in range(n_blk)], axis=0)
            y_parts.append(u[:, sl] * s_g)

    def out_proj():
        y = _bf16(jnp.concatenate(y_parts, axis=1))
        o_ref[0, rows, :] = x_ref[...] + _dot(y, w_out_ref[...])

    return [functools.partial(head, hd) for hd in range(RET_HEADS)] + [gating, out_proj]


def _mixer_kernel(x_first_ref, x_odd_ref, x_even_ref, g_mix_ref, w_in_ref,
                  cos_ref, sin_ref, dmat_ref, qdec_ref, kdec_ref,
                  ret_g_ref, sg_g_ref, w_s_ref, b_s_ref, w_out_ref, o_ref,
                  state_ref, z_even_ref, z_odd_ref, xk_even_ref, xk_odd_ref, *, pairs_per_row):
    ts = x_odd_ref.shape[1]
    k = pl.program_id(0)
    project = lambda x_ref, z_ref, xk_ref: _project_parts(
        x_ref, g_mix_ref, w_in_ref, z_ref, xk_ref)
    pair_pos = pl.multiple_of((k % pairs_per_row) * (2 * ts), 2 * ts)
    mix = lambda z_ref, xk_ref, half: _mix_parts(
        z_ref, xk_ref, slice(half * ts, (half + 1) * ts), pl.ds(pair_pos + half * ts, ts),
        cos_ref, sin_ref, dmat_ref, qdec_ref, kdec_ref,
        ret_g_ref, sg_g_ref, w_s_ref, b_s_ref, w_out_ref, state_ref, o_ref)

    def interleave(proj_parts, mix_parts):
        for proj_part, mix_part in zip(proj_parts, mix_parts, strict=True):
            proj_part()
            mix_part()

    @pl.when(k == 0)
    def _():
        for proj_part in project(x_first_ref, z_even_ref, xk_even_ref):
            proj_part()

    @pl.when(k % pairs_per_row == 0)
    def _():
        state_ref[...] = jnp.zeros_like(state_ref)

    interleave(project(x_odd_ref, z_odd_ref, xk_odd_ref),
               mix(z_even_ref, xk_even_ref, 0))
    interleave(project(x_even_ref, z_even_ref, xk_even_ref),
               mix(z_odd_ref, xk_odd_ref, 1))


def _ffn_kernel(x_ref, p_ref, g_ffn_ref, w1_ref, w2_ref, g_ple_ref, w_gate_ref, w_ple_ref,
                g_final_ref, o_ref):
    n_sub = x_ref.shape[0] // FFN_SUB_ROWS
    subs = [slice(i * FFN_SUB_ROWS, (i + 1) * FFN_SUB_ROWS) for i in range(n_sub)]
    xs = [x_ref[r, :] for r in subs]
    hn = [_bf16(x * _rms_scale(x) * g_ffn_ref[...]) for x in xs]
    hf = [jnp.maximum(_dot(h, w1_ref[...]), 0.0) for h in hn]
    xs = [x + _dot(_bf16(h * h), w2_ref[...]) for x, h in zip(xs, hf)]
    gn = [_bf16(x * _rms_scale(x) * g_ple_ref[...]) for x in xs]
    gate = [jax.nn.sigmoid(_dot(g, w_gate_ref[...])) for g in gn]
    ple = [_dot(_bf16(p_ref[r, :]), w_ple_ref[...]) for r in subs]
    for r, x, g, e in zip(subs, xs, gate, ple):
        x = x + g * e
        o_ref[r, :] = x * _rms_scale(x) * g_final_ref[...]


def _const_spec(shape):
    zeros = (0,) * len(shape)
    return pl.BlockSpec(shape, lambda *_: zeros, pipeline_mode=pl.Buffered(1))


def _retention_tables(seq, rb):
    f32 = jnp.float32
    pos = jnp.arange(seq, dtype=f32)
    freqs = ROPE_BASE ** (-jnp.arange(0, HEAD_DIM, 2, dtype=f32) / HEAD_DIM)
    ang = pos[:, None] * freqs[None, :]
    cos, sin = jnp.cos(ang), jnp.sin(ang)
    cos = jnp.concatenate([cos, cos], axis=1)
    sin = jnp.concatenate([-sin, sin], axis=1)
    log_g = jnp.log(jnp.asarray(_GAMMAS, f32))
    t = jnp.arange(rb, dtype=f32)
    dist = jnp.abs(t[:, None] - t[None, :])
    visible = (t[None, :] // CHUNK) <= (t[:, None] // CHUNK)
    dmat = jnp.where(visible[None], jnp.exp(dist[None] * log_g[:, None, None]), 0.0)
    qdec = jnp.exp((t[None, :] + 1.0) * log_g[:, None])
    kdec = jnp.exp((rb - 1.0 - t)[None, :] * log_g[:, None])
    bcast = lambda a: jnp.broadcast_to(a[:, :, None], (RET_HEADS, rb, HEAD_DIM))
    return cos, sin, dmat, bcast(qdec), bcast(kdec)


def kernel(x, p, g_mix, w_in, ret_norm_g, sg_norm_g, w_s, b_s, w_out, g_ffn, w_ff1, w_ff2,
           g_ple, w_ple_gate, w_ple, g_final):
    batch, seq, d = x.shape
    ts, tm, rb = SEQ_TILE, ROW_TILE, RET_BLOCK
    assert w_in.shape[0] == 1, "one layer: the final RMSNorm is fused into its channel mixer"
    assert d == D_MODEL and seq % ts == 0 and (batch * seq) % tm == 0
    cos, sin, dmat, qdec, kdec = _retention_tables(seq, rb)
    tiles_per_row = seq // ts
    assert tiles_per_row % 2 == 0, "a grid step mixes a pair of seq tiles of one batch row"
    pairs_per_row = tiles_per_row // 2
    n_tiles = batch * tiles_per_row

    def x_tile_spec(tile_of_step):
        def index(k):
            t = jnp.minimum(tile_of_step(k), n_tiles - 1)
            return (t // tiles_per_row, t % tiles_per_row, 0)
        return pl.BlockSpec((1, ts, d), index)

    pair_spec = pl.BlockSpec((1, 2 * ts, d), lambda k: (k // pairs_per_row, k % pairs_per_row, 0))
    f32_tile = lambda w: pltpu.VMEM((ts, w), jnp.float32)

    x = pl.pallas_call(
        functools.partial(_mixer_kernel, pairs_per_row=pairs_per_row),
        grid=(n_tiles // 2,),
        in_specs=[pl.BlockSpec((1, ts, d), lambda k: (0, 0, 0), pipeline_mode=pl.Buffered(1)),
                  x_tile_spec(lambda k: 2 * k + 1), x_tile_spec(lambda k: 2 * k + 2),
                  _const_spec((1, d)), _const_spec((d, IN_WIDTH)),
                  _const_spec((seq, HEAD_DIM)), _const_spec((seq, HEAD_DIM)),
                  _const_spec((RET_HEADS, rb, rb)),
                  _const_spec((RET_HEADS, rb, HEAD_DIM)),
                  _const_spec((RET_HEADS, rb, HEAD_DIM)),
                  _const_spec((1, RET_WIDTH)), _const_spec((1, SG_WIDTH)),
                  _const_spec((SG_GROUPS, SG_BLOCK, SG_BLOCK)),
                  _const_spec((SG_BLOCK, SG_GROUPS)),
                  _const_spec((d, d))],
        out_specs=pair_spec,
        out_shape=jax.ShapeDtypeStruct((batch, seq, d), jnp.float32),
        scratch_shapes=[pltpu.VMEM((RET_HEADS, HEAD_DIM, HEAD_DIM), jnp.float32),
                        f32_tile(IN_WIDTH), f32_tile(IN_WIDTH), f32_tile(d), f32_tile(d)],
        compiler_params=pltpu.CompilerParams(
            dimension_semantics=("arbitrary",),
            vmem_limit_bytes=V7X_VMEM_LIMIT_BYTES),
        name="token_mixer",
    )(x, x, x, g_mix[0][None], _bf16(w_in[0]), cos, sin, dmat, qdec, kdec,
      ret_norm_g[0][None], sg_norm_g[0][None], w_s[0], b_s[0].T, _bf16(w_out[0]))

    row_spec = lambda w: pl.BlockSpec((tm, w), lambda i: (i, 0))
    x = pl.pallas_call(
        _ffn_kernel,
        grid=(batch * seq // tm,),
        in_specs=[row_spec(d), row_spec(PLE_DIM), _const_spec((1, d)),
                  _const_spec((d, D_FF)), _const_spec((D_FF, d)), _const_spec((1, d)),
                  _const_spec((d, d)), _const_spec((PLE_DIM, d)), _const_spec((1, d))],
        out_specs=row_spec(d),
        out_shape=jax.ShapeDtypeStruct((batch * seq, d), jnp.float32),
        compiler_params=pltpu.CompilerParams(
            dimension_semantics=("arbitrary",),
            vmem_limit_bytes=V7X_VMEM_LIMIT_BYTES),
        name="channel_mixer",
    )(x.reshape(batch * seq, d), p[0].reshape(batch * seq, PLE_DIM), g_ffn[0][None],
      _bf16(w_ff1[0]), _bf16(w_ff2[0]), g_ple[0][None], _bf16(w_ple_gate[0]),
      _bf16(w_ple[0]), g_final[None])
    return x.reshape(batch, seq, d)
```

```python
import functools

import jax
import jax.numpy as jnp
from jax import lax
from jax.experimental import pallas as pl
from jax.experimental.pallas import tpu as pltpu

D_MODEL = 1024
CHUNK = 64
RET_HEADS = 4
HEAD_DIM = 128
RET_WIDTH = RET_HEADS * HEAD_DIM
SG_BLOCK = 128
SG_GROUPS = 4
SG_DIM = 128
SG_WIDTH = SG_GROUPS * SG_DIM
IN_WIDTH = 4 * RET_WIDTH + 2 * SG_WIDTH
D_FF = 4 * D_MODEL
PLE_DIM = 256
ROPE_BASE = 10000.0
EPS = 1e-6

LANES = 128
SEQ_TILE = 512
RET_BLOCK = 256
ROW_TILE = 1024
FFN_SUB_ROWS = 256
PROJ_PARTS = 6
V7X_VMEM_LIMIT_BYTES = 56 * 1024 * 1024

_GAMMAS = tuple(1.0 - 2.0 ** (-5.0 - h) for h in range(RET_HEADS))


def _bf16(a):
    return a.astype(jnp.bfloat16)


def _dot(a, b):
    return jnp.dot(a, b, preferred_element_type=jnp.float32)


def _rms_scale(x):
    return lax.rsqrt(jnp.mean(x * x, axis=-1, keepdims=True) + EPS)


def _layernorm_nogain(x):
    mu = jnp.mean(x, axis=-1, keepdims=True)
    xc = x - mu
    return xc * lax.rsqrt(jnp.mean(xc * xc, axis=-1, keepdims=True) + EPS)


def _project_parts(x_ref, g_mix_ref, w_in_ref, z_ref, xkeep_ref):
    x = x_ref[0]
    xkeep_ref[...] = x
    h = _bf16(x * _rms_scale(x) * g_mix_ref[...])
    width = IN_WIDTH // PROJ_PARTS

    def part(c):
        zc = _dot(h, w_in_ref[:, c * width:(c + 1) * width])
        for j in range(width // LANES):
            z_ref[c * (width // LANES) + j] = zc[:, j * LANES:(j + 1) * LANES]

    return [functools.partial(part, c) for c in range(PROJ_PARTS)]


def _mix_parts(z_ref, x_ref, rows, cos_ref, sin_ref, dmat_ref, qdec_ref, kdec_ref,
               ret_g_ref, sg_g_ref, w_s_ref, b_s_ref, w_out_ref, state_ref, o_ref):
    ts = z_ref.shape[1]
    y_parts = []
    slab = lambda part, idx: z_ref[part * RET_HEADS + idx]
    wide = lambda part: jnp.concatenate([slab(part, g) for g in range(SG_GROUPS)], axis=1)

    def head(hd):
        cos, sin = cos_ref[rows, :], sin_ref[rows, :]
        qh, kh, vh, gh = slab(0, hd), slab(1, hd), slab(2, hd), slab(3, hd)
        qr = qh * cos + pltpu.roll(qh, HEAD_DIM // 2, 1) * sin
        kr = (kh * cos + pltpu.roll(kh, HEAD_DIM // 2, 1) * sin) * HEAD_DIM ** -0.5
        vb = _bf16(vh)
        o_blocks = []
        for blk in range(ts // RET_BLOCK):
            brows = slice(blk * RET_BLOCK, (blk + 1) * RET_BLOCK)
            qb, kb, vbb = qr[brows], kr[brows], vb[brows]
            scores = lax.dot_general(_bf16(qb), _bf16(kb), (((1,), (1,)), ((), ())),
                                     preferred_element_type=jnp.float32)
            o = _dot(_bf16(scores * dmat_ref[hd]), vbb)
            o_blocks.append(o + _dot(_bf16(qb * qdec_ref[hd]), _bf16(state_ref[hd])))
            kv = lax.dot_general(_bf16(kb * kdec_ref[hd]), vbb, (((0,), (0,)), ((), ())),
                                 preferred_element_type=jnp.float32)
            state_ref[hd] = state_ref[hd] * (_GAMMAS[hd] ** RET_BLOCK) + kv
        o = jnp.concatenate(o_blocks, axis=0)
        r = _layernorm_nogain(o) * ret_g_ref[:, hd * HEAD_DIM:(hd + 1) * HEAD_DIM]
        y_parts.append(jax.nn.silu(gh) * r)

    def gating():
        sv = jax.nn.gelu(wide(5))
        svn = _bf16(_layernorm_nogain(sv) * sg_g_ref[...])
        row = lax.broadcasted_iota(jnp.int32, (SG_BLOCK, SG_BLOCK), 0)
        col = lax.broadcasted_iota(jnp.int32, (SG_BLOCK, SG_BLOCK), 1)
        allowed = (col // CHUNK) <= (row // CHUNK)
        n_blk = ts // SG_BLOCK
        for g in range(SG_GROUPS):
            w_g = _bf16(jnp.where(allowed, w_s_ref[g], 0.0))
            sl = slice(g * SG_DIM, (g + 1) * SG_DIM)
            vcat = jnp.concatenate(
                [svn[b * SG_BLOCK:(b + 1) * SG_BLOCK, sl] for b in range(n_blk)], axis=1)
            s_cat = _dot(w_g, vcat) + b_s_ref[:, g:g + 1]
            y_parts.append(jnp.concatenate(
                [s_cat[:, b * SG_DIM:(b + 1) * SG_DIM] for b in range(n_blk)], axis=0))

    def out_proj():
        for g in range(SG_GROUPS):
            y_parts[RET_HEADS + g] = jax.nn.gelu(slab(4, g)) * y_parts[RET_HEADS + g]
        y = _bf16(jnp.concatenate(y_parts, axis=1))
        o_ref[0, rows, :] = x_ref[...] + _dot(y, w_out_ref[...])

    return [functools.partial(head, hd) for hd in range(RET_HEADS)] + [gating, out_proj]


def _mixer_kernel(x_first_ref, x_odd_ref, x_even_ref, g_mix_ref, w_in_ref,
                  cos_ref, sin_ref, dmat_ref, qdec_ref, kdec_ref,
                  ret_g_ref, sg_g_ref, w_s_ref, b_s_ref, w_out_ref, o_ref,
                  state_ref, z_even_ref, z_odd_ref, xk_even_ref, xk_odd_ref, *, pairs_per_row):
    ts = x_odd_ref.shape[1]
    k = pl.program_id(0)
    project = lambda x_ref, z_ref, xk_ref: _project_parts(
        x_ref, g_mix_ref, w_in_ref, z_ref, xk_ref)
    mix = lambda z_ref, xk_ref, rows: _mix_parts(
        z_ref, xk_ref, rows, cos_ref, sin_ref, dmat_ref, qdec_ref, kdec_ref,
        ret_g_ref, sg_g_ref, w_s_ref, b_s_ref, w_out_ref, state_ref, o_ref)

    def interleave(proj_parts, mix_parts):
        for proj_part, mix_part in zip(proj_parts, mix_parts, strict=True):
            proj_part()
            mix_part()

    @pl.when(k == 0)
    def _():
        for proj_part in project(x_first_ref, z_even_ref, xk_even_ref):
            proj_part()

    @pl.when(k % pairs_per_row == 0)
    def _():
        state_ref[...] = jnp.zeros_like(state_ref)

    interleave(project(x_odd_ref, z_odd_ref, xk_odd_ref),
               mix(z_even_ref, xk_even_ref, slice(0, ts)))
    interleave(project(x_even_ref, z_even_ref, xk_even_ref),
               mix(z_odd_ref, xk_odd_ref, slice(ts, 2 * ts)))


def _ffn_kernel(x_ref, p_ref, g_ffn_ref, w1_ref, w2_ref, g_ple_ref, w_gate_ref, w_ple_ref,
                g_final_ref, o_ref):
    n_sub = x_ref.shape[0] // FFN_SUB_ROWS
    subs = [slice(i * FFN_SUB_ROWS, (i + 1) * FFN_SUB_ROWS) for i in range(n_sub)]
    xs = [x_ref[r, :] for r in subs]
    hn = [_bf16(x * _rms_scale(x) * g_ffn_ref[...]) for x in xs]
    hf = [jnp.maximum(_dot(h, w1_ref[...]), 0.0) for h in hn]
    xs = [x + _dot(_bf16(h * h), w2_ref[...]) for x, h in zip(xs, hf)]
    gn = [_bf16(x * _rms_scale(x) * g_ple_ref[...]) for x in xs]
    gate = [jax.nn.sigmoid(_dot(g, w_gate_ref[...])) for g in gn]
    ple = [_dot(_bf16(p_ref[r, :]), w_ple_ref[...]) for r in subs]
    for r, x, g, e in zip(subs, xs, gate, ple):
        x = x + g * e
        o_ref[r, :] = x * _rms_scale(x) * g_final_ref[...]


def _const_spec(shape):
    zeros = (0,) * len(shape)
    return pl.BlockSpec(shape, lambda *_: zeros, pipeline_mode=pl.Buffered(1))


def _retention_tables(seq, rb):
    f32 = jnp.float32
    pos = jnp.arange(seq, dtype=f32)
    freqs = ROPE_BASE ** (-jnp.arange(0, HEAD_DIM, 2, dtype=f32) / HEAD_DIM)
    ang = pos[:, None] * freqs[None, :]
    cos, sin = jnp.cos(ang), jnp.sin(ang)
    cos = jnp.concatenate([cos, cos], axis=1)
    sin = jnp.concatenate([-sin, sin], axis=1)
    log_g = jnp.log(jnp.asarray(_GAMMAS, f32))
    t = jnp.arange(rb, dtype=f32)
    dist = jnp.abs(t[:, None] - t[None, :])
    visible = (t[None, :] // CHUNK) <= (t[:, None] // CHUNK)
    dmat = jnp.where(visible[None], jnp.exp(dist[None] * log_g[:, None, None]), 0.0)
    qdec = jnp.exp((t[None, :] + 1.0) * log_g[:, None])
    kdec = jnp.exp((rb - 1.0 - t)[None, :] * log_g[:, None])
    bcast = lambda a: jnp.broadcast_to(a[:, :, None], (RET_HEADS, rb, HEAD_DIM))
    return cos, sin, dmat, bcast(qdec), bcast(kdec)


def kernel(x, p, g_mix, w_in, ret_norm_g, sg_norm_g, w_s, b_s, w_out, g_ffn, w_ff1, w_ff2,
           g_ple, w_ple_gate, w_ple, g_final):
    batch, seq, d = x.shape
    ts, tm, rb = SEQ_TILE, ROW_TILE, RET_BLOCK
    assert w_in.shape[0] == 1, "one layer: the final RMSNorm is fused into its channel mixer"
    assert d == D_MODEL and seq % ts == 0 and (batch * seq) % tm == 0
    cos, sin, dmat, qdec, kdec = _retention_tables(seq, rb)
    tiles_per_row = seq // ts
    assert tiles_per_row % 2 == 0, "a grid step mixes a pair of seq tiles of one batch row"
    pairs_per_row = tiles_per_row // 2
    n_tiles = batch * tiles_per_row

    def x_tile_spec(tile_of_step):
        def index(k):
            t = jnp.minimum(tile_of_step(k), n_tiles - 1)
            return (t // tiles_per_row, t % tiles_per_row, 0)
        return pl.BlockSpec((1, ts, d), index)

    tab_spec = pl.BlockSpec((2 * ts, HEAD_DIM), lambda k: (k % pairs_per_row, 0))
    pair_spec = pl.BlockSpec((1, 2 * ts, d), lambda k: (k // pairs_per_row, k % pairs_per_row, 0))
    z_slabs = pltpu.VMEM((IN_WIDTH // LANES, ts, LANES), jnp.float32)
    x_keep = pltpu.VMEM((ts, d), jnp.float32)

    x = pl.pallas_call(
        functools.partial(_mixer_kernel, pairs_per_row=pairs_per_row),
        grid=(n_tiles // 2,),
        in_specs=[pl.BlockSpec((1, ts, d), lambda k: (0, 0, 0), pipeline_mode=pl.Buffered(1)),
                  x_tile_spec(lambda k: 2 * k + 1), x_tile_spec(lambda k: 2 * k + 2),
                  _const_spec((1, d)), _const_spec((d, IN_WIDTH)),
                  tab_spec, tab_spec,
                  _const_spec((RET_HEADS, rb, rb)),
                  _const_spec((RET_HEADS, rb, HEAD_DIM)),
                  _const_spec((RET_HEADS, rb, HEAD_DIM)),
                  _const_spec((1, RET_WIDTH)), _const_spec((1, SG_WIDTH)),
                  _const_spec((SG_GROUPS, SG_BLOCK, SG_BLOCK)),
                  _const_spec((SG_BLOCK, SG_GROUPS)),
                  _const_spec((d, d))],
        out_specs=pair_spec,
        out_shape=jax.ShapeDtypeStruct((batch, seq, d), jnp.float32),
        scratch_shapes=[pltpu.VMEM((RET_HEADS, HEAD_DIM, HEAD_DIM), jnp.float32),
                        z_slabs, z_slabs, x_keep, x_keep],
        compiler_params=pltpu.CompilerParams(
            dimension_semantics=("arbitrary",),
            vmem_limit_bytes=V7X_VMEM_LIMIT_BYTES),
        name="token_mixer",
    )(x, x, x, g_mix[0][None], _bf16(w_in[0]), cos, sin, dmat, qdec, kdec,
      ret_norm_g[0][None], sg_norm_g[0][None], w_s[0], b_s[0].T, _bf16(w_out[0]))

    row_spec = lambda w: pl.BlockSpec((tm, w), lambda i: (i, 0))
    x = pl.pallas_call(
        _ffn_kernel,
        grid=(batch * seq // tm,),
        in_specs=[row_spec(d), row_spec(PLE_DIM), _const_spec((1, d)),
                  _const_spec((d, D_FF)), _const_spec((D_FF, d)), _const_spec((1, d)),
                  _const_spec((d, d)), _const_spec((PLE_DIM, d)), _const_spec((1, d))],
        out_specs=row_spec(d),
        out_shape=jax.ShapeDtypeStruct((batch * seq, d), jnp.float32),
        compiler_params=pltpu.CompilerParams(
            dimension_semantics=("arbitrary",),
            vmem_limit_bytes=V7X_VMEM_LIMIT_BYTES),
        name="channel_mixer",
    )(x.reshape(batch * seq, d), p[0].reshape(batch * seq, PLE_DIM), g_ffn[0][None],
      _bf16(w_ff1[0]), _bf16(w_ff2[0]), g_ple[0][None], _bf16(w_ple_gate[0]),
      _bf16(w_ple[0]), g_final[None])
    return x.reshape(batch, seq, d)
```

```python
import functools

import jax
import jax.numpy as jnp
from jax import lax
from jax.experimental import pallas as pl
from jax.experimental.pallas import tpu as pltpu

D_MODEL = 1024
CHUNK = 64
RET_HEADS = 4
HEAD_DIM = 128
RET_WIDTH = RET_HEADS * HEAD_DIM
SG_BLOCK = 128
SG_GROUPS = 4
SG_DIM = 128
SG_WIDTH = SG_GROUPS * SG_DIM
IN_WIDTH = 4 * RET_WIDTH + 2 * SG_WIDTH
D_FF = 4 * D_MODEL
PLE_DIM = 256
ROPE_BASE = 10000.0
EPS = 1e-6

LANES = 128
BF16_SUBLANES = 16
SEQ_TILE = 512
RET_BLOCK = 256
ROW_TILE = 1024
FFN_SUB_ROWS = 256
PROJ_PARTS = 6
V7X_VMEM_LIMIT_BYTES = 56 * 1024 * 1024

_GAMMAS = tuple(1.0 - 2.0 ** (-5.0 - h) for h in range(RET_HEADS))


def _bf16(a):
    return a.astype(jnp.bfloat16)


def _dot(a, b):
    return jnp.dot(a, b, preferred_element_type=jnp.float32)


def _rms_scale(x):
    return lax.rsqrt(jnp.mean(x * x, axis=-1, keepdims=True) + EPS)


def _layernorm_nogain(x):
    mu = jnp.mean(x, axis=-1, keepdims=True)
    xc = x - mu
    return xc * lax.rsqrt(jnp.mean(xc * xc, axis=-1, keepdims=True) + EPS)


def _project_parts(x_ref, g_mix_ref, w_in_ref, z_ref, xkeep_ref):
    x = x_ref[0]
    xkeep_ref[...] = x
    h = _bf16(x * _rms_scale(x) * g_mix_ref[...])
    width = IN_WIDTH // PROJ_PARTS

    def part(c):
        zc = _dot(h, w_in_ref[:, c * width:(c + 1) * width])
        for j in range(width // LANES):
            z_ref[c * (width // LANES) + j] = zc[:, j * LANES:(j + 1) * LANES]

    return [functools.partial(part, c) for c in range(PROJ_PARTS)]


def _mix_parts(z_ref, x_ref, rows, cos_ref, sin_ref, dmat_ref, qdec_ref, kdec_ref,
               ret_g_ref, sg_g_ref, w_s_ref, b_s_ref, w_out_ref, state_ref, o_ref):
    ts = z_ref.shape[1]
    y_parts = []
    slab = lambda part, idx: z_ref[part * RET_HEADS + idx]
    wide = lambda part: jnp.concatenate([slab(part, g) for g in range(SG_GROUPS)], axis=1)

    def head(hd):
        cos, sin = cos_ref[rows, :], sin_ref[rows, :]
        qh, kh, vh, gh = slab(0, hd), slab(1, hd), slab(2, hd), slab(3, hd)
        qr = qh * cos + pltpu.roll(qh, HEAD_DIM // 2, 1) * sin
        kr = (kh * cos + pltpu.roll(kh, HEAD_DIM // 2, 1) * sin) * HEAD_DIM ** -0.5
        vb = _bf16(vh)
        o_blocks = []
        for blk in range(ts // RET_BLOCK):
            brows = slice(blk * RET_BLOCK, (blk + 1) * RET_BLOCK)
            qb, kb, vbb = qr[brows], kr[brows], vb[brows]
            scores = lax.dot_general(_bf16(qb), _bf16(kb), (((1,), (1,)), ((), ())),
                                     preferred_element_type=jnp.float32)
            o = _dot(_bf16(scores * dmat_ref[hd]), vbb)
            o_blocks.append(o + _dot(_bf16(qb * qdec_ref[hd]), _bf16(state_ref[hd])))
            kv = lax.dot_general(_bf16(kb * kdec_ref[hd]), vbb, (((0,), (0,)), ((), ())),
                                 preferred_element_type=jnp.float32)
            state_ref[hd] = state_ref[hd] * (_GAMMAS[hd] ** RET_BLOCK) + kv
        o = jnp.concatenate(o_blocks, axis=0)
        r = _layernorm_nogain(o) * ret_g_ref[:, hd * HEAD_DIM:(hd + 1) * HEAD_DIM]
        y_parts.append(jax.nn.silu(gh) * r)

    def gating():
        sv = jax.nn.gelu(wide(5))
        svn = _bf16(_layernorm_nogain(sv) * sg_g_ref[...])
        row = lax.broadcasted_iota(jnp.int32, (SG_BLOCK, SG_BLOCK), 0)
        col = lax.broadcasted_iota(jnp.int32, (SG_BLOCK, SG_BLOCK), 1)
        allowed = (col // CHUNK) <= (row // CHUNK)
        n_blk = ts // SG_BLOCK
        for g in range(SG_GROUPS):
            w_g = _bf16(jnp.where(allowed, w_s_ref[g], 0.0))
            sl = slice(g * SG_DIM, (g + 1) * SG_DIM)
            vcat = jnp.concatenate(
                [svn[b * SG_BLOCK:(b + 1) * SG_BLOCK, sl] for b in range(n_blk)], axis=1)
            s_cat = _dot(w_g, vcat) + b_s_ref[:, g:g + 1]
            y_parts.append(jnp.concatenate(
                [s_cat[:, b * SG_DIM:(b + 1) * SG_DIM] for b in range(n_blk)], axis=0))

    def out_proj():
        for g in range(SG_GROUPS):
            y_parts[RET_HEADS + g] = jax.nn.gelu(slab(4, g)) * y_parts[RET_HEADS + g]
        y = _bf16(jnp.concatenate(y_parts, axis=1))
        o_ref[0, rows, :] = x_ref[...] + _dot(y, w_out_ref[...])

    return [functools.partial(head, hd) for hd in range(RET_HEADS)] + [gating, out_proj]


def _mixer_kernel(x_first_ref, x_odd_ref, x_even_ref, g_mix_ref, w_in_ref,
                  cos_ref, sin_ref, dmat_ref, qdec_ref, kdec_ref,
                  ret_g_ref, sg_g_ref, w_s_ref, b_s_ref, w_out_ref,
                  w_ff1_ref, w_ff2_ref, w_gate_ref,
                  o_ref, w_ff1_bf16_ref, w_ff2_bf16_ref, w_gate_bf16_ref,
                  state_ref, z_even_ref, z_odd_ref, xk_even_ref, xk_odd_ref, *, pairs_per_row):
    w_ff1_bf16_ref[...] = _bf16(w_ff1_ref[...])
    w_ff2_bf16_ref[...] = _bf16(w_ff2_ref[...])
    w_gate_bf16_ref[...] = _bf16(w_gate_ref[...])

    ts = x_odd_ref.shape[1]
    k = pl.program_id(0)
    project = lambda x_ref, z_ref, xk_ref: _project_parts(
        x_ref, g_mix_ref, w_in_ref, z_ref, xk_ref)
    mix = lambda z_ref, xk_ref, rows: _mix_parts(
        z_ref, xk_ref, rows, cos_ref, sin_ref, dmat_ref, qdec_ref, kdec_ref,
        ret_g_ref, sg_g_ref, w_s_ref, b_s_ref, w_out_ref, state_ref, o_ref)

    def interleave(proj_parts, mix_parts):
        for proj_part, mix_part in zip(proj_parts, mix_parts, strict=True):
            proj_part()
            mix_part()

    @pl.when(k == 0)
    def _():
        for proj_part in project(x_first_ref, z_even_ref, xk_even_ref):
            proj_part()

    @pl.when(k % pairs_per_row == 0)
    def _():
        state_ref[...] = jnp.zeros_like(state_ref)

    interleave(project(x_odd_ref, z_odd_ref, xk_odd_ref),
               mix(z_even_ref, xk_even_ref, slice(0, ts)))
    interleave(project(x_even_ref, z_even_ref, xk_even_ref),
               mix(z_odd_ref, xk_odd_ref, slice(ts, 2 * ts)))


def _ffn_kernel(x_ref, p_ref, g_ffn_ref, w1_ref, w2_ref, g_ple_ref, w_gate_ref, w_ple_ref,
                g_final_ref, o_ref):
    n_sub = x_ref.shape[0] // FFN_SUB_ROWS
    subs = [slice(i * FFN_SUB_ROWS, (i + 1) * FFN_SUB_ROWS) for i in range(n_sub)]
    xs = [x_ref[r, :] for r in subs]
    hn = [_bf16(x * _rms_scale(x) * g_ffn_ref[...]) for x in xs]
    hf = [jnp.maximum(_dot(h, w1_ref[...]), 0.0) for h in hn]
    xs = [x + _dot(_bf16(h * h), w2_ref[...]) for x, h in zip(xs, hf)]
    ple = [_dot(_bf16(p_ref[r, :]), w_ple_ref[...]) for r in subs]
    gn = [_bf16(x * _rms_scale(x) * g_ple_ref[...]) for x in xs]
    gate = [jax.nn.sigmoid(_dot(g, w_gate_ref[...])) for g in gn]
    for r, x, g, e in zip(subs, xs, gate, ple):
        x = x + g * e
        o_ref[r, :] = x * _rms_scale(x) * g_final_ref[...]


def _const_spec(shape):
    zeros = (0,) * len(shape)
    return pl.BlockSpec(shape, lambda *_: zeros, pipeline_mode=pl.Buffered(1))


def _retention_tables(seq, rb):
    f32 = jnp.float32
    pos = jnp.arange(seq, dtype=f32)
    freqs = ROPE_BASE ** (-jnp.arange(0, HEAD_DIM, 2, dtype=f32) / HEAD_DIM)
    ang = pos[:, None] * freqs[None, :]
    cos, sin = jnp.cos(ang), jnp.sin(ang)
    cos = jnp.concatenate([cos, cos], axis=1)
    sin = jnp.concatenate([-sin, sin], axis=1)
    log_g = jnp.log(jnp.asarray(_GAMMAS, f32))
    t = jnp.arange(rb, dtype=f32)
    dist = jnp.abs(t[:, None] - t[None, :])
    visible = (t[None, :] // CHUNK) <= (t[:, None] // CHUNK)
    dmat = jnp.where(visible[None], jnp.exp(dist[None] * log_g[:, None, None]), 0.0)
    qdec = jnp.exp((t[None, :] + 1.0) * log_g[:, None])
    kdec = jnp.exp((rb - 1.0 - t)[None, :] * log_g[:, None])
    bcast = lambda a: jnp.broadcast_to(a[:, :, None], (RET_HEADS, rb, HEAD_DIM))
    return cos, sin, dmat, bcast(qdec), bcast(kdec)


def kernel(x, p, g_mix, w_in, ret_norm_g, sg_norm_g, w_s, b_s, w_out, g_ffn, w_ff1, w_ff2,
           g_ple, w_ple_gate, w_ple, g_final):
    batch, seq, d = x.shape
    ts, tm, rb = SEQ_TILE, ROW_TILE, RET_BLOCK
    assert w_in.shape[0] == 1, "one layer: the final RMSNorm is fused into its channel mixer"
    assert d == D_MODEL and seq % ts == 0 and (batch * seq) % tm == 0
    cos, sin, dmat, qdec, kdec = _retention_tables(seq, rb)
    tiles_per_row = seq // ts
    assert tiles_per_row % 2 == 0, "a grid step mixes a pair of seq tiles of one batch row"
    pairs_per_row = tiles_per_row // 2
    n_tiles = batch * tiles_per_row

    def x_tile_spec(tile_of_step):
        def index(k):
            t = jnp.minimum(tile_of_step(k), n_tiles - 1)
            return (t // tiles_per_row, t % tiles_per_row, 0)
        return pl.BlockSpec((1, ts, d), index)

    tab_spec = pl.BlockSpec((2 * ts, HEAD_DIM), lambda k: (k % pairs_per_row, 0))
    pair_spec = pl.BlockSpec((1, 2 * ts, d), lambda k: (k // pairs_per_row, k % pairs_per_row, 0))
    z_slabs = pltpu.VMEM((IN_WIDTH // LANES, ts, LANES), jnp.float32)
    n_steps = n_tiles // 2

    def row_chunk(w):
        rows, cols = w.shape
        assert rows % (n_steps * BF16_SUBLANES) == 0, "a bf16 block needs whole packed sublane tiles"
        return pl.BlockSpec((rows // n_steps, cols), lambda k: (k, 0))

    ffn_weights = (w_ff1[0], w_ff2[0], w_ple_gate[0])
    x_keep = pltpu.VMEM((ts, d), jnp.float32)

    x, w_ff1_bf16, w_ff2_bf16, w_gate_bf16 = pl.pallas_call(
        functools.partial(_mixer_kernel, pairs_per_row=pairs_per_row),
        grid=(n_steps,),
        in_specs=[pl.BlockSpec((1, ts, d), lambda k: (0, 0, 0), pipeline_mode=pl.Buffered(1)),
                  x_tile_spec(lambda k: 2 * k + 1), x_tile_spec(lambda k: 2 * k + 2),
                  _const_spec((1, d)), _const_spec((d, IN_WIDTH)),
                  tab_spec, tab_spec,
                  _const_spec((RET_HEADS, rb, rb)),
                  _const_spec((RET_HEADS, rb, HEAD_DIM)),
                  _const_spec((RET_HEADS, rb, HEAD_DIM)),
                  _const_spec((1, RET_WIDTH)), _const_spec((1, SG_WIDTH)),
                  _const_spec((SG_GROUPS, SG_BLOCK, SG_BLOCK)),
                  _const_spec((SG_BLOCK, SG_GROUPS)),
                  _const_spec((d, d))] + [row_chunk(w) for w in ffn_weights],
        out_specs=[pair_spec] + [row_chunk(w) for w in ffn_weights],
        out_shape=[jax.ShapeDtypeStruct((batch, seq, d), jnp.float32)]
        + [jax.ShapeDtypeStruct(w.shape, jnp.bfloat16) for w in ffn_weights],
        scratch_shapes=[pltpu.VMEM((RET_HEADS, HEAD_DIM, HEAD_DIM), jnp.float32),
                        z_slabs, z_slabs, x_keep, x_keep],
        compiler_params=pltpu.CompilerParams(
            dimension_semantics=("arbitrary",),
            vmem_limit_bytes=V7X_VMEM_LIMIT_BYTES),
        name="token_mixer",
    )(x, x, x, g_mix[0][None], _bf16(w_in[0]), cos, sin, dmat, qdec, kdec,
      ret_norm_g[0][None], sg_norm_g[0][None], w_s[0], b_s[0].T, _bf16(w_out[0]), *ffn_weights)

    row_spec = lambda w: pl.BlockSpec((tm, w), lambda i: (i, 0))
    x = pl.pallas_call(
        _ffn_kernel,
        grid=(batch * seq // tm,),
        in_specs=[row_spec(d), row_spec(PLE_DIM), _const_spec((1, d)),
                  _const_spec((d, D_FF)), _const_spec((D_FF, d)), _const_spec((1, d)),
                  _const_spec((d, d)), _const_spec((PLE_DIM, d)), _const_spec((1, d))],
        out_specs=row_spec(d),
        out_shape=jax.ShapeDtypeStruct((batch * seq, d), jnp.float32),
        compiler_params=pltpu.CompilerParams(
            dimension_semantics=("arbitrary",),
            vmem_limit_bytes=V7X_VMEM_LIMIT_BYTES),
        name="channel_mixer",
    )(x.reshape(batch * seq, d), p[0].reshape(batch * seq, PLE_DIM), g_ffn[0][None],
      w_ff1_bf16, w_ff2_bf16, g_ple[0][None], w_gate_bf16, _bf16(w_ple[0]), g_final[None])
    return x.reshape(batch, seq, d)
```

```python
import functools

import jax
import jax.numpy as jnp
import numpy as np
from jax import lax
from jax.experimental import pallas as pl
from jax.experimental.pallas import tpu as pltpu

D_MODEL = 1024
CHUNK = 64
RET_HEADS = 4
HEAD_DIM = 128
RET_WIDTH = RET_HEADS * HEAD_DIM
SG_BLOCK = 128
SG_GROUPS = 4
SG_DIM = 128
SG_WIDTH = SG_GROUPS * SG_DIM
IN_WIDTH = 4 * RET_WIDTH + 2 * SG_WIDTH
D_FF = 4 * D_MODEL
PLE_DIM = 256
ROPE_BASE = 10000.0
EPS = 1e-6

LANES = 128
BF16_SUBLANES = 16
SEQ_TILE = 512
RET_BLOCK = 256
ROW_TILE = 1024
FFN_SUB_ROWS = 256
PROJ_PARTS = 6
V7X_VMEM_LIMIT_BYTES = 56 * 1024 * 1024

_GAMMAS = tuple(1.0 - 2.0 ** (-5.0 - h) for h in range(RET_HEADS))


def _bf16(a):
    return a.astype(jnp.bfloat16)


def _dot(a, b):
    return jnp.dot(a, b, preferred_element_type=jnp.float32)


def _rms_scale(x):
    return lax.rsqrt(jnp.mean(x * x, axis=-1, keepdims=True) + EPS)


def _layernorm_nogain(x):
    mu = jnp.mean(x, axis=-1, keepdims=True)
    xc = x - mu
    return xc * lax.rsqrt(jnp.mean(xc * xc, axis=-1, keepdims=True) + EPS)


def _project_parts(x_ref, g_mix_ref, w_in_ref, z_ref, xkeep_ref):
    x = x_ref[0]
    xkeep_ref[...] = x
    h = _bf16(x * _rms_scale(x) * g_mix_ref[...])
    width = IN_WIDTH // PROJ_PARTS

    def part(c):
        zc = _dot(h, w_in_ref[:, c * width:(c + 1) * width])
        for j in range(width // LANES):
            z_ref[c * (width // LANES) + j] = zc[:, j * LANES:(j + 1) * LANES]

    return [functools.partial(part, c) for c in range(PROJ_PARTS)]


def _mix_parts(z_ref, x_ref, rows, cos_ref, sin_ref, dmat_ref, qdec_ref, kdec_ref,
               ret_g_ref, sg_g_ref, w_s_ref, b_s_ref, w_out_ref, state_ref, o_ref):
    ts = z_ref.shape[1]
    y_parts = []
    slab = lambda part, idx: z_ref[part * RET_HEADS + idx]
    wide = lambda part: jnp.concatenate([slab(part, g) for g in range(SG_GROUPS)], axis=1)

    def head(hd):
        cos, sin = cos_ref[rows, :], sin_ref[rows, :]
        qh, kh, vh, gh = slab(0, hd), slab(1, hd), slab(2, hd), slab(3, hd)
        qr = qh * cos + pltpu.roll(qh, HEAD_DIM // 2, 1) * sin
        kr = (kh * cos + pltpu.roll(kh, HEAD_DIM // 2, 1) * sin) * HEAD_DIM ** -0.5
        vb = _bf16(vh)
        o_blocks = []
        for blk in range(ts // RET_BLOCK):
            brows = slice(blk * RET_BLOCK, (blk + 1) * RET_BLOCK)
            qb, kb, vbb = qr[brows], kr[brows], vb[brows]
            scores = lax.dot_general(_bf16(qb), _bf16(kb), (((1,), (1,)), ((), ())),
                                     preferred_element_type=jnp.float32)
            o = _dot(_bf16(scores * dmat_ref[hd]), vbb)
            o_blocks.append(o + _dot(_bf16(qb * qdec_ref[hd]), _bf16(state_ref[hd])))
            kv = lax.dot_general(_bf16(kb * kdec_ref[hd]), vbb, (((0,), (0,)), ((), ())),
                                 preferred_element_type=jnp.float32)
            state_ref[hd] = state_ref[hd] * (_GAMMAS[hd] ** RET_BLOCK) + kv
        o = jnp.concatenate(o_blocks, axis=0)
        r = _layernorm_nogain(o) * ret_g_ref[:, hd * HEAD_DIM:(hd + 1) * HEAD_DIM]
        y_parts.append(jax.nn.silu(gh) * r)

    def gating():
        sv = jax.nn.gelu(wide(5))
        svn = _bf16(_layernorm_nogain(sv) * sg_g_ref[...])
        row = lax.broadcasted_iota(jnp.int32, (SG_BLOCK, SG_BLOCK), 0)
        col = lax.broadcasted_iota(jnp.int32, (SG_BLOCK, SG_BLOCK), 1)
        allowed = (col // CHUNK) <= (row // CHUNK)
        n_blk = ts // SG_BLOCK
        for g in range(SG_GROUPS):
            w_g = _bf16(jnp.where(allowed, w_s_ref[g], 0.0))
            sl = slice(g * SG_DIM, (g + 1) * SG_DIM)
            vcat = jnp.concatenate(
                [svn[b * SG_BLOCK:(b + 1) * SG_BLOCK, sl] for b in range(n_blk)], axis=1)
            s_cat = _dot(w_g, vcat) + b_s_ref[:, g:g + 1]
            y_parts.append(jnp.concatenate(
                [s_cat[:, b * SG_DIM:(b + 1) * SG_DIM] for b in range(n_blk)], axis=0))

    def out_proj():
        for g in range(SG_GROUPS):
            y_parts[RET_HEADS + g] = jax.nn.gelu(slab(4, g)) * y_parts[RET_HEADS + g]
        y = _bf16(jnp.concatenate(y_parts, axis=1))
        o_ref[0, rows, :] = x_ref[...] + _dot(y, w_out_ref[...])

    return [functools.partial(head, hd) for hd in range(RET_HEADS)] + [gating, out_proj]


def _mixer_kernel(x_first_ref, x_odd_ref, x_even_ref, g_mix_ref, w_in_ref,
                  cos_ref, sin_ref, dmat_ref, qdec_ref, kdec_ref,
                  ret_g_ref, sg_g_ref, w_s_ref, b_s_ref, w_out_ref,
                  w_ff1_ref, w_ff2_ref, w_gate_ref,
                  o_ref, w_ff1_bf16_ref, w_ff2_bf16_ref, w_gate_bf16_ref,
                  state_ref, z_even_ref, z_odd_ref, xk_even_ref, xk_odd_ref, *, pairs_per_row):
    w_ff1_bf16_ref[...] = _bf16(w_ff1_ref[...])
    w_ff2_bf16_ref[...] = _bf16(w_ff2_ref[...])
    w_gate_bf16_ref[...] = _bf16(w_gate_ref[...])

    ts = x_odd_ref.shape[1]
    k = pl.program_id(0)
    project = lambda x_ref, z_ref, xk_ref: _project_parts(
        x_ref, g_mix_ref, w_in_ref, z_ref, xk_ref)
    mix = lambda z_ref, xk_ref, rows: _mix_parts(
        z_ref, xk_ref, rows, cos_ref, sin_ref, dmat_ref, qdec_ref, kdec_ref,
        ret_g_ref, sg_g_ref, w_s_ref, b_s_ref, w_out_ref, state_ref, o_ref)

    def interleave(proj_parts, mix_parts):
        for proj_part, mix_part in zip(proj_parts, mix_parts, strict=True):
            proj_part()
            mix_part()

    @pl.when(k == 0)
    def _():
        for proj_part in project(x_first_ref, z_even_ref, xk_even_ref):
            proj_part()

    @pl.when(k % pairs_per_row == 0)
    def _():
        state_ref[...] = jnp.zeros_like(state_ref)

    interleave(project(x_odd_ref, z_odd_ref, xk_odd_ref),
               mix(z_even_ref, xk_even_ref, slice(0, ts)))
    interleave(project(x_even_ref, z_even_ref, xk_even_ref),
               mix(z_odd_ref, xk_odd_ref, slice(ts, 2 * ts)))


def _ffn_kernel(x_ref, p_ref, g_ffn_ref, w1_ref, w2_ref, g_ple_ref, w_gate_ref, w_ple_ref,
                g_final_ref, o_ref):
    n_sub = x_ref.shape[0] // FFN_SUB_ROWS
    subs = [slice(i * FFN_SUB_ROWS, (i + 1) * FFN_SUB_ROWS) for i in range(n_sub)]
    ple_dot = lambda r: _dot(_bf16(p_ref[r, :]), w_ple_ref[...])
    ple = [ple_dot(r) for r in subs[:n_sub // 2]]
    xs = [x_ref[r, :] for r in subs]
    hn = [_bf16(x * _rms_scale(x) * g_ffn_ref[...]) for x in xs]
    hf = [jnp.maximum(_dot(h, w1_ref[...]), 0.0) for h in hn]
    xs = [x + _dot(_bf16(h * h), w2_ref[...]) for x, h in zip(xs, hf)]
    ple += [ple_dot(r) for r in subs[n_sub // 2:-1]]
    gn = [_bf16(x * _rms_scale(x) * g_ple_ref[...]) for x in xs]
    gate = [jax.nn.sigmoid(_dot(g, w_gate_ref[...])) for g in gn]
    ple.append(ple_dot(subs[-1]))
    for r, x, g, e in zip(subs, xs, gate, ple):
        x = x + g * e
        o_ref[r, :] = x * _rms_scale(x) * g_final_ref[...]


def _const_spec(shape):
    zeros = (0,) * len(shape)
    return pl.BlockSpec(shape, lambda *_: zeros, pipeline_mode=pl.Buffered(1))


def _retention_tables(seq, rb):
    pos = np.arange(seq, dtype=np.float64)
    freqs = ROPE_BASE ** (-np.arange(0, HEAD_DIM, 2, dtype=np.float64) / HEAD_DIM)
    ang = pos[:, None] * freqs[None, :]
    cos = np.concatenate([np.cos(ang), np.cos(ang)], axis=1)
    sin = np.concatenate([-np.sin(ang), np.sin(ang)], axis=1)
    log_g = np.log(np.asarray(_GAMMAS, np.float64))
    t = np.arange(rb, dtype=np.float64)
    dist = np.abs(t[:, None] - t[None, :])
    visible = (t[None, :] // CHUNK) <= (t[:, None] // CHUNK)
    dmat = np.where(visible[None], np.exp(dist[None] * log_g[:, None, None]), 0.0)
    qdec = np.exp((t[None, :] + 1.0) * log_g[:, None])
    kdec = np.exp((rb - 1.0 - t)[None, :] * log_g[:, None])
    bcast = lambda a: np.broadcast_to(a[:, :, None], (RET_HEADS, rb, HEAD_DIM))
    return tuple(jnp.asarray(a, jnp.float32) for a in (cos, sin, dmat, bcast(qdec), bcast(kdec)))


def kernel(x, p, g_mix, w_in, ret_norm_g, sg_norm_g, w_s, b_s, w_out, g_ffn, w_ff1, w_ff2,
           g_ple, w_ple_gate, w_ple, g_final):
    batch, seq, d = x.shape
    ts, tm, rb = SEQ_TILE, ROW_TILE, RET_BLOCK
    assert w_in.shape[0] == 1, "one layer: the final RMSNorm is fused into its channel mixer"
    assert d == D_MODEL and seq % ts == 0 and (batch * seq) % tm == 0
    cos, sin, dmat, qdec, kdec = _retention_tables(seq, rb)
    tiles_per_row = seq // ts
    assert tiles_per_row % 2 == 0, "a grid step mixes a pair of seq tiles of one batch row"
    pairs_per_row = tiles_per_row // 2
    n_tiles = batch * tiles_per_row

    def x_tile_spec(tile_of_step):
        def index(k):
            t = jnp.minimum(tile_of_step(k), n_tiles - 1)
            return (t // tiles_per_row, t % tiles_per_row, 0)
        return pl.BlockSpec((1, ts, d), index)

    tab_spec = pl.BlockSpec((2 * ts, HEAD_DIM), lambda k: (k % pairs_per_row, 0))
    pair_spec = pl.BlockSpec((1, 2 * ts, d), lambda k: (k // pairs_per_row, k % pairs_per_row, 0))
    z_slabs = pltpu.VMEM((IN_WIDTH // LANES, ts, LANES), jnp.float32)
    n_steps = n_tiles // 2

    def row_chunk(w):
        rows, cols = w.shape
        assert rows % (n_steps * BF16_SUBLANES) == 0, "a bf16 block needs whole packed sublane tiles"
        return pl.BlockSpec((rows // n_steps, cols), lambda k: (k, 0))

    ffn_weights = (w_ff1[0], w_ff2[0], w_ple_gate[0])
    x_keep = pltpu.VMEM((ts, d), jnp.float32)

    x, w_ff1_bf16, w_ff2_bf16, w_gate_bf16 = pl.pallas_call(
        functools.partial(_mixer_kernel, pairs_per_row=pairs_per_row),
        grid=(n_steps,),
        in_specs=[pl.BlockSpec((1, ts, d), lambda k: (0, 0, 0), pipeline_mode=pl.Buffered(1)),
                  x_tile_spec(lambda k: 2 * k + 1), x_tile_spec(lambda k: 2 * k + 2),
                  _const_spec((1, d)), _const_spec((d, IN_WIDTH)),
                  tab_spec, tab_spec,
                  _const_spec((RET_HEADS, rb, rb)),
                  _const_spec((RET_HEADS, rb, HEAD_DIM)),
                  _const_spec((RET_HEADS, rb, HEAD_DIM)),
                  _const_spec((1, RET_WIDTH)), _const_spec((1, SG_WIDTH)),
                  _const_spec((SG_GROUPS, SG_BLOCK, SG_BLOCK)),
                  _const_spec((SG_BLOCK, SG_GROUPS)),
                  _const_spec((d, d))] + [row_chunk(w) for w in ffn_weights],
        out_specs=[pair_spec] + [row_chunk(w) for w in ffn_weights],
        out_shape=[jax.ShapeDtypeStruct((batch, seq, d), jnp.float32)]
        + [jax.ShapeDtypeStruct(w.shape, jnp.bfloat16) for w in ffn_weights],
        scratch_shapes=[pltpu.VMEM((RET_HEADS, HEAD_DIM, HEAD_DIM), jnp.float32),
                        z_slabs, z_slabs, x_keep, x_keep],
        compiler_params=pltpu.CompilerParams(
            dimension_semantics=("arbitrary",),
            vmem_limit_bytes=V7X_VMEM_LIMIT_BYTES),
        name="token_mixer",
    )(x, x, x, g_mix[0][None], _bf16(w_in[0]), cos, sin, dmat, qdec, kdec,
      ret_norm_g[0][None], sg_norm_g[0][None], w_s[0], b_s[0].T, _bf16(w_out[0]), *ffn_weights)

    row_spec = lambda w: pl.BlockSpec((tm, w), lambda i: (i, 0))
    x = pl.pallas_call(
        _ffn_kernel,
        grid=(batch * seq // tm,),
        in_specs=[row_spec(d), row_spec(PLE_DIM), _const_spec((1, d)),
                  _const_spec((d, D_FF)), _const_spec((D_FF, d)), _const_spec((1, d)),
                  _const_spec((d, d)), _const_spec((PLE_DIM, d)), _const_spec((1, d))],
        out_specs=row_spec(d),
        out_shape=jax.ShapeDtypeStruct((batch * seq, d), jnp.float32),
        compiler_params=pltpu.CompilerParams(
            dimension_semantics=("arbitrary",),
            vmem_limit_bytes=V7X_VMEM_LIMIT_BYTES),
        name="channel_mixer",
    )(x.reshape(batch * seq, d), p[0].reshape(batch * seq, PLE_DIM), g_ffn[0][None],
      w_ff1_bf16, w_ff2_bf16, g_ple[0][None], w_gate_bf16, _bf16(w_ple[0]), g_final[None])
    return x.reshape(batch, seq, d)
```

```python
import functools

import jax
import jax.numpy as jnp
import numpy as np
from jax import lax
from jax.experimental import pallas as pl
from jax.experimental.pallas import tpu as pltpu

D_MODEL = 1024
CHUNK = 64
RET_HEADS = 4
HEAD_DIM = 128
RET_WIDTH = RET_HEADS * HEAD_DIM
SG_BLOCK = 128
SG_GROUPS = 4
SG_DIM = 128
SG_WIDTH = SG_GROUPS * SG_DIM
IN_WIDTH = 4 * RET_WIDTH + 2 * SG_WIDTH
D_FF = 4 * D_MODEL
PLE_DIM = 256
ROPE_BASE = 10000.0
EPS = 1e-6

LANES = 128
BF16_SUBLANES = 16
F32_SUBLANES = 8
SEQ_TILE = 512
RET_BLOCK = 256
ROW_TILE = 1024
FFN_SUB_ROWS = 256
PROJ_PARTS = 6
V7X_VMEM_LIMIT_BYTES = 56 * 1024 * 1024

_GAMMAS = tuple(1.0 - 2.0 ** (-5.0 - h) for h in range(RET_HEADS))


def _bf16(a):
    return a.astype(jnp.bfloat16)


def _dot(a, b):
    return jnp.dot(a, b, preferred_element_type=jnp.float32)


def _rms_scale(x):
    return lax.rsqrt(jnp.mean(x * x, axis=-1, keepdims=True) + EPS)


def _layernorm_nogain(x):
    mu = jnp.mean(x, axis=-1, keepdims=True)
    xc = x - mu
    return xc * lax.rsqrt(jnp.mean(xc * xc, axis=-1, keepdims=True) + EPS)


def _project_parts(x_ref, g_mix_ref, w_in_ref, z_ref, xkeep_ref):
    x = x_ref[0]
    xkeep_ref[...] = x
    h = _bf16(x * _rms_scale(x) * g_mix_ref[0:1, :])
    width = IN_WIDTH // PROJ_PARTS

    def part(c):
        zc = _dot(h, w_in_ref[:, c * width:(c + 1) * width])
        for j in range(width // LANES):
            z_ref[c * (width // LANES) + j] = zc[:, j * LANES:(j + 1) * LANES]

    return [functools.partial(part, c) for c in range(PROJ_PARTS)]


def _mix_parts(z_ref, x_ref, rows, cos_ref, sin_ref, dmat_ref, qdec_ref, kdec_ref,
               ret_g_ref, sg_g_ref, w_s_ref, b_s_ref, w_out_ref, state_ref, o_ref):
    ts = z_ref.shape[1]
    y_parts = []
    slab = lambda part, idx: z_ref[part * RET_HEADS + idx]
    wide = lambda part: jnp.concatenate([slab(part, g) for g in range(SG_GROUPS)], axis=1)

    def head(hd):
        cos, sin = cos_ref[rows, :], sin_ref[rows, :]
        qh, kh, vh, gh = slab(0, hd), slab(1, hd), slab(2, hd), slab(3, hd)
        qr = qh * cos + pltpu.roll(qh, HEAD_DIM // 2, 1) * sin
        kr = (kh * cos + pltpu.roll(kh, HEAD_DIM // 2, 1) * sin) * HEAD_DIM ** -0.5
        vb = _bf16(vh)
        o_blocks = []
        for blk in range(ts // RET_BLOCK):
            brows = slice(blk * RET_BLOCK, (blk + 1) * RET_BLOCK)
            qb, kb, vbb = qr[brows], kr[brows], vb[brows]
            scores = lax.dot_general(_bf16(qb), _bf16(kb), (((1,), (1,)), ((), ())),
                                     preferred_element_type=jnp.float32)
            o = _dot(_bf16(scores * dmat_ref[hd]), vbb)
            o_blocks.append(o + _dot(_bf16(qb * qdec_ref[hd]), _bf16(state_ref[hd])))
            kv = lax.dot_general(_bf16(kb * kdec_ref[hd]), vbb, (((0,), (0,)), ((), ())),
                                 preferred_element_type=jnp.float32)
            state_ref[hd] = state_ref[hd] * (_GAMMAS[hd] ** RET_BLOCK) + kv
        o = jnp.concatenate(o_blocks, axis=0)
        r = _layernorm_nogain(o) * ret_g_ref[0:1, hd * HEAD_DIM:(hd + 1) * HEAD_DIM]
        y_parts.append(jax.nn.silu(gh) * r)

    def gating():
        sv = jax.nn.gelu(wide(5))
        svn = _bf16(_layernorm_nogain(sv) * sg_g_ref[0:1, :])
        row = lax.broadcasted_iota(jnp.int32, (SG_BLOCK, SG_BLOCK), 0)
        col = lax.broadcasted_iota(jnp.int32, (SG_BLOCK, SG_BLOCK), 1)
        allowed = (col // CHUNK) <= (row // CHUNK)
        n_blk = ts // SG_BLOCK
        for g in range(SG_GROUPS):
            w_g = _bf16(jnp.where(allowed, w_s_ref[g], 0.0))
            sl = slice(g * SG_DIM, (g + 1) * SG_DIM)
            vcat = jnp.concatenate(
                [svn[b * SG_BLOCK:(b + 1) * SG_BLOCK, sl] for b in range(n_blk)], axis=1)
            s_cat = _dot(w_g, vcat) + b_s_ref[:, g:g + 1]
            y_parts.append(jnp.concatenate(
                [s_cat[:, b * SG_DIM:(b + 1) * SG_DIM] for b in range(n_blk)], axis=0))

    def out_proj():
        for g in range(SG_GROUPS):
            y_parts[RET_HEADS + g] = jax.nn.gelu(slab(4, g)) * y_parts[RET_HEADS + g]
        y = _bf16(jnp.concatenate(y_parts, axis=1))
        o_ref[0, rows, :] = x_ref[...] + _dot(y, w_out_ref[...])

    return [functools.partial(head, hd) for hd in range(RET_HEADS)] + [gating, out_proj]


def _mixer_kernel(x_first_ref, x_odd_ref, x_even_ref, g_mix_ref, w_in_ref,
                  cos_ref, sin_ref, dmat_ref, qdec_ref, kdec_ref,
                  ret_g_ref, sg_g_ref, w_s_ref, b_s_ref, w_out_ref,
                  w_ff1_ref, w_ff2_ref, w_gate_ref,
                  o_ref, w_ff1_bf16_ref, w_ff2_bf16_ref, w_gate_bf16_ref,
                  state_ref, z_even_ref, z_odd_ref, xk_even_ref, xk_odd_ref, *, pairs_per_row):
    w_ff1_bf16_ref[...] = _bf16(w_ff1_ref[...])
    w_ff2_bf16_ref[...] = _bf16(w_ff2_ref[...])
    w_gate_bf16_ref[...] = _bf16(w_gate_ref[...])

    ts = x_odd_ref.shape[1]
    k = pl.program_id(0)
    project = lambda x_ref, z_ref, xk_ref: _project_parts(
        x_ref, g_mix_ref, w_in_ref, z_ref, xk_ref)
    mix = lambda z_ref, xk_ref, rows: _mix_parts(
        z_ref, xk_ref, rows, cos_ref, sin_ref, dmat_ref, qdec_ref, kdec_ref,
        ret_g_ref, sg_g_ref, w_s_ref, b_s_ref, w_out_ref, state_ref, o_ref)

    def interleave(proj_parts, mix_parts):
        for proj_part, mix_part in zip(proj_parts, mix_parts, strict=True):
            proj_part()
            mix_part()

    @pl.when(k == 0)
    def _():
        for proj_part in project(x_first_ref, z_even_ref, xk_even_ref):
            proj_part()

    @pl.when(k % pairs_per_row == 0)
    def _():
        state_ref[...] = jnp.zeros_like(state_ref)

    interleave(project(x_odd_ref, z_odd_ref, xk_odd_ref),
               mix(z_even_ref, xk_even_ref, slice(0, ts)))
    interleave(project(x_even_ref, z_even_ref, xk_even_ref),
               mix(z_odd_ref, xk_odd_ref, slice(ts, 2 * ts)))


def _ffn_kernel(x_ref, p_ref, g_ffn_ref, w1_ref, w2_ref, g_ple_ref, w_gate_ref, w_ple_ref,
                g_final_ref, o_ref):
    n_sub = x_ref.shape[0] // FFN_SUB_ROWS
    subs = [slice(i * FFN_SUB_ROWS, (i + 1) * FFN_SUB_ROWS) for i in range(n_sub)]
    ple_dot = lambda r: _dot(_bf16(p_ref[r, :]), w_ple_ref[...])
    ple = [ple_dot(r) for r in subs[:n_sub // 2]]
    xs = [x_ref[r, :] for r in subs]
    hn = [_bf16(x * _rms_scale(x) * g_ffn_ref[0:1, :]) for x in xs]
    hf = [jnp.maximum(_dot(h, w1_ref[...]), 0.0) for h in hn]
    xs = [x + _dot(_bf16(h * h), w2_ref[...]) for x, h in zip(xs, hf)]
    ple += [ple_dot(r) for r in subs[n_sub // 2:-1]]
    gn = [_bf16(x * _rms_scale(x) * g_ple_ref[0:1, :]) for x in xs]
    gate = [jax.nn.sigmoid(_dot(g, w_gate_ref[...])) for g in gn]
    ple.append(ple_dot(subs[-1]))
    for r, x, g, e in zip(subs, xs, gate, ple):
        x = x + g * e
        o_ref[r, :] = x * _rms_scale(x) * g_final_ref[0:1, :]


def _gain_rows(g):
    return jnp.broadcast_to(g[None], (F32_SUBLANES, g.shape[0]))


def _const_spec(shape):
    zeros = (0,) * len(shape)
    return pl.BlockSpec(shape, lambda *_: zeros, pipeline_mode=pl.Buffered(1))


def _retention_tables(seq, rb):
    pos = np.arange(seq, dtype=np.float64)
    freqs = ROPE_BASE ** (-np.arange(0, HEAD_DIM, 2, dtype=np.float64) / HEAD_DIM)
    ang = pos[:, None] * freqs[None, :]
    cos = np.concatenate([np.cos(ang), np.cos(ang)], axis=1)
    sin = np.concatenate([-np.sin(ang), np.sin(ang)], axis=1)
    log_g = np.log(np.asarray(_GAMMAS, np.float64))
    t = np.arange(rb, dtype=np.float64)
    dist = np.abs(t[:, None] - t[None, :])
    visible = (t[None, :] // CHUNK) <= (t[:, None] // CHUNK)
    dmat = np.where(visible[None], np.exp(dist[None] * log_g[:, None, None]), 0.0)
    qdec = np.exp((t[None, :] + 1.0) * log_g[:, None])
    kdec = np.exp((rb - 1.0 - t)[None, :] * log_g[:, None])
    bcast = lambda a: np.broadcast_to(a[:, :, None], (RET_HEADS, rb, HEAD_DIM))
    return tuple(jnp.asarray(a, jnp.float32) for a in (cos, sin, dmat, bcast(qdec), bcast(kdec)))


def kernel(x, p, g_mix, w_in, ret_norm_g, sg_norm_g, w_s, b_s, w_out, g_ffn, w_ff1, w_ff2,
           g_ple, w_ple_gate, w_ple, g_final):
    batch, seq, d = x.shape
    ts, tm, rb = SEQ_TILE, ROW_TILE, RET_BLOCK
    assert w_in.shape[0] == 1, "one layer: the final RMSNorm is fused into its channel mixer"
    assert d == D_MODEL and seq % ts == 0 and (batch * seq) % tm == 0
    cos, sin, dmat, qdec, kdec = _retention_tables(seq, rb)
    tiles_per_row = seq // ts
    assert tiles_per_row % 2 == 0, "a grid step mixes a pair of seq tiles of one batch row"
    pairs_per_row = tiles_per_row // 2
    n_tiles = batch * tiles_per_row

    def x_tile_spec(tile_of_step):
        def index(k):
            t = jnp.minimum(tile_of_step(k), n_tiles - 1)
            return (t // tiles_per_row, t % tiles_per_row, 0)
        return pl.BlockSpec((1, ts, d), index)

    tab_spec = pl.BlockSpec((2 * ts, HEAD_DIM), lambda k: (k % pairs_per_row, 0))
    pair_spec = pl.BlockSpec((1, 2 * ts, d), lambda k: (k // pairs_per_row, k % pairs_per_row, 0))
    z_slabs = pltpu.VMEM((IN_WIDTH // LANES, ts, LANES), jnp.float32)
    n_steps = n_tiles // 2

    def row_chunk(w):
        rows, cols = w.shape
        assert rows % (n_steps * BF16_SUBLANES) == 0, "a bf16 block needs whole packed sublane tiles"
        return pl.BlockSpec((rows // n_steps, cols), lambda k: (k, 0))

    ffn_weights = (w_ff1[0], w_ff2[0], w_ple_gate[0])
    x_keep = pltpu.VMEM((ts, d), jnp.float32)

    x, w_ff1_bf16, w_ff2_bf16, w_gate_bf16 = pl.pallas_call(
        functools.partial(_mixer_kernel, pairs_per_row=pairs_per_row),
        grid=(n_steps,),
        in_specs=[pl.BlockSpec((1, ts, d), lambda k: (0, 0, 0), pipeline_mode=pl.Buffered(1)),
                  x_tile_spec(lambda k: 2 * k + 1), x_tile_spec(lambda k: 2 * k + 2),
                  _const_spec((F32_SUBLANES, d)), _const_spec((d, IN_WIDTH)),
                  tab_spec, tab_spec,
                  _const_spec((RET_HEADS, rb, rb)),
                  _const_spec((RET_HEADS, rb, HEAD_DIM)),
                  _const_spec((RET_HEADS, rb, HEAD_DIM)),
                  _const_spec((F32_SUBLANES, RET_WIDTH)), _const_spec((F32_SUBLANES, SG_WIDTH)),
                  _const_spec((SG_GROUPS, SG_BLOCK, SG_BLOCK)),
                  _const_spec((SG_BLOCK, SG_GROUPS)),
                  _const_spec((d, d))] + [row_chunk(w) for w in ffn_weights],
        out_specs=[pair_spec] + [row_chunk(w) for w in ffn_weights],
        out_shape=[jax.ShapeDtypeStruct((batch, seq, d), jnp.float32)]
        + [jax.ShapeDtypeStruct(w.shape, jnp.bfloat16) for w in ffn_weights],
        scratch_shapes=[pltpu.VMEM((RET_HEADS, HEAD_DIM, HEAD_DIM), jnp.float32),
                        z_slabs, z_slabs, x_keep, x_keep],
        compiler_params=pltpu.CompilerParams(
            dimension_semantics=("arbitrary",),
            vmem_limit_bytes=V7X_VMEM_LIMIT_BYTES),
        name="token_mixer",
    )(x, x, x, _gain_rows(g_mix[0]), _bf16(w_in[0]), cos, sin, dmat, qdec, kdec,
      _gain_rows(ret_norm_g[0]), _gain_rows(sg_norm_g[0]), w_s[0], b_s[0].T, _bf16(w_out[0]), *ffn_weights)

    row_spec = lambda w: pl.BlockSpec((tm, w), lambda i: (i, 0))
    x = pl.pallas_call(
        _ffn_kernel,
        grid=(batch * seq // tm,),
        in_specs=[row_spec(d), row_spec(PLE_DIM), _const_spec((F32_SUBLANES, d)),
                  _const_spec((d, D_FF)), _const_spec((D_FF, d)), _const_spec((F32_SUBLANES, d)),
                  _const_spec((d, d)), _const_spec((PLE_DIM, d)), _const_spec((F32_SUBLANES, d))],
        out_specs=row_spec(d),
        out_shape=jax.ShapeDtypeStruct((batch * seq, d), jnp.float32),
        compiler_params=pltpu.CompilerParams(
            dimension_semantics=("arbitrary",),
            vmem_limit_bytes=V7X_VMEM_LIMIT_BYTES),
        name="channel_mixer",
    )(x.reshape(batch * seq, d), p[0].reshape(batch * seq, PLE_DIM), _gain_rows(g_ffn[0]),
      w_ff1_bf16, w_ff2_bf16, _gain_rows(g_ple[0]), w_gate_bf16, _bf16(w_ple[0]),
      _gain_rows(g_final))
    return x.reshape(batch, seq, d)
```

```python
import functools

import jax
import jax.numpy as jnp
import numpy as np
from jax import lax
from jax.experimental import pallas as pl
from jax.experimental.pallas import tpu as pltpu

D_MODEL = 1024
CHUNK = 64
RET_HEADS = 4
HEAD_DIM = 128
RET_WIDTH = RET_HEADS * HEAD_DIM
SG_BLOCK = 128
SG_GROUPS = 4
SG_DIM = 128
SG_WIDTH = SG_GROUPS * SG_DIM
IN_WIDTH = 4 * RET_WIDTH + 2 * SG_WIDTH
D_FF = 4 * D_MODEL
PLE_DIM = 256
ROPE_BASE = 10000.0
EPS = 1e-6

LANES = 128
BF16_SUBLANES = 16
SEQ_TILE = 512
RET_BLOCK = 256
ROW_TILE = 512
FFN_SUB_ROWS = 256
PROJ_PARTS = 6
V7X_VMEM_LIMIT_BYTES = 56 * 1024 * 1024
V7X_FFN_VMEM_LIMIT_BYTES = 62 * 1024 * 1024

_GAMMAS = tuple(1.0 - 2.0 ** (-5.0 - h) for h in range(RET_HEADS))


def _bf16(a):
    return a.astype(jnp.bfloat16)


def _dot(a, b):
    return jnp.dot(a, b, preferred_element_type=jnp.float32)


def _rms_scale(x):
    return lax.rsqrt(jnp.mean(x * x, axis=-1, keepdims=True) + EPS)


def _layernorm_nogain(x):
    mu = jnp.mean(x, axis=-1, keepdims=True)
    xc = x - mu
    return xc * lax.rsqrt(jnp.mean(xc * xc, axis=-1, keepdims=True) + EPS)


def _project_parts(x_ref, g_mix_ref, w_in_ref, z_ref):
    x = x_ref[0]
    h = _bf16(x * _rms_scale(x) * g_mix_ref[...])
    width = IN_WIDTH // PROJ_PARTS

    def part(c):
        zc = _dot(h, w_in_ref[:, c * width:(c + 1) * width])
        for j in range(width // LANES):
            z_ref[c * (width // LANES) + j] = zc[:, j * LANES:(j + 1) * LANES]

    return [functools.partial(part, c) for c in range(PROJ_PARTS)]


def _mix_parts(z_ref, rows, cos_ref, sin_ref, dmat_ref, qdec_ref, kdec_ref,
               ret_g_ref, sg_g_ref, w_s_ref, b_s_ref, state_ref, o_ref):
    ts = z_ref.shape[1]
    y_parts = []
    slab = lambda part, idx: z_ref[part * RET_HEADS + idx]
    wide = lambda part: jnp.concatenate([slab(part, g) for g in range(SG_GROUPS)], axis=1)

    def head(hd):
        cos, sin = cos_ref[rows, :], sin_ref[rows, :]
        qh, kh, vh, gh = slab(0, hd), slab(1, hd), slab(2, hd), slab(3, hd)
        qr = qh * cos + pltpu.roll(qh, HEAD_DIM // 2, 1) * sin
        kr = (kh * cos + pltpu.roll(kh, HEAD_DIM // 2, 1) * sin) * HEAD_DIM ** -0.5
        vb = _bf16(vh)
        o_blocks = []
        for blk in range(ts // RET_BLOCK):
            brows = slice(blk * RET_BLOCK, (blk + 1) * RET_BLOCK)
            qb, kb, vbb = qr[brows], kr[brows], vb[brows]
            scores = lax.dot_general(_bf16(qb), _bf16(kb), (((1,), (1,)), ((), ())),
                                     preferred_element_type=jnp.float32)
            o = _dot(_bf16(scores * dmat_ref[hd]), vbb)
            o_blocks.append(o + _dot(_bf16(qb * qdec_ref[hd]), _bf16(state_ref[hd])))
            kv = lax.dot_general(_bf16(kb * kdec_ref[hd]), vbb, (((0,), (0,)), ((), ())),
                                 preferred_element_type=jnp.float32)
            state_ref[hd] = state_ref[hd] * (_GAMMAS[hd] ** RET_BLOCK) + kv
        o = jnp.concatenate(o_blocks, axis=0)
        r = _layernorm_nogain(o) * ret_g_ref[:, hd * HEAD_DIM:(hd + 1) * HEAD_DIM]
        y_parts.append(jax.nn.silu(gh) * r)

    def gating():
        sv = jax.nn.gelu(wide(5))
        svn = _bf16(_layernorm_nogain(sv) * sg_g_ref[...])
        row = lax.broadcasted_iota(jnp.int32, (SG_BLOCK, SG_BLOCK), 0)
        col = lax.broadcasted_iota(jnp.int32, (SG_BLOCK, SG_BLOCK), 1)
        allowed = (col // CHUNK) <= (row // CHUNK)
        n_blk = ts // SG_BLOCK
        for g in range(SG_GROUPS):
            w_g = _bf16(jnp.where(allowed, w_s_ref[g], 0.0))
            sl = slice(g * SG_DIM, (g + 1) * SG_DIM)
            vcat = jnp.concatenate(
                [svn[b * SG_BLOCK:(b + 1) * SG_BLOCK, sl] for b in range(n_blk)], axis=1)
            s_cat = _dot(w_g, vcat) + b_s_ref[:, g:g + 1]
            y_parts.append(jnp.concatenate(
                [s_cat[:, b * SG_DIM:(b + 1) * SG_DIM] for b in range(n_blk)], axis=0))

    def store():
        for g in range(SG_GROUPS):
            y_parts[RET_HEADS + g] = jax.nn.gelu(slab(4, g)) * y_parts[RET_HEADS + g]
        o_ref[0, rows, :] = _bf16(jnp.concatenate(y_parts, axis=1))

    return [functools.partial(head, hd) for hd in range(RET_HEADS)] + [gating, store]


def _mixer_kernel(x_first_ref, x_odd_ref, x_even_ref, g_mix_ref, w_in_ref,
                  cos_ref, sin_ref, dmat_ref, qdec_ref, kdec_ref,
                  ret_g_ref, sg_g_ref, w_s_ref, b_s_ref,
                  w_ff1_ref, w_ff2_ref, w_gate_ref,
                  o_ref, w_ff1_bf16_ref, w_ff2_bf16_ref, w_gate_bf16_ref,
                  state_ref, z_even_ref, z_odd_ref, *, pairs_per_row):
    w_ff1_bf16_ref[...] = _bf16(w_ff1_ref[...])
    w_ff2_bf16_ref[...] = _bf16(w_ff2_ref[...])
    w_gate_bf16_ref[...] = _bf16(w_gate_ref[...])

    ts = x_odd_ref.shape[1]
    k = pl.program_id(0)
    project = lambda x_ref, z_ref: _project_parts(x_ref, g_mix_ref, w_in_ref, z_ref)
    mix = lambda z_ref, rows: _mix_parts(
        z_ref, rows, cos_ref, sin_ref, dmat_ref, qdec_ref, kdec_ref,
        ret_g_ref, sg_g_ref, w_s_ref, b_s_ref, state_ref, o_ref)

    def interleave(proj_parts, mix_parts):
        for proj_part, mix_part in zip(proj_parts, mix_parts, strict=True):
            proj_part()
            mix_part()

    @pl.when(k == 0)
    def _():
        for proj_part in project(x_first_ref, z_even_ref):
            proj_part()

    @pl.when(k % pairs_per_row == 0)
    def _():
        state_ref[...] = jnp.zeros_like(state_ref)

    interleave(project(x_odd_ref, z_odd_ref), mix(z_even_ref, slice(0, ts)))
    interleave(project(x_even_ref, z_even_ref), mix(z_odd_ref, slice(ts, 2 * ts)))


def _ffn_kernel(x_ref, y_ref, p_ref, w_out_ref, g_ffn_ref, w1_ref, w2_ref, g_ple_ref, w_gate_ref,
                w_ple_ref, g_final_ref, o_ref):
    n_sub = x_ref.shape[0] // FFN_SUB_ROWS
    subs = [slice(i * FFN_SUB_ROWS, (i + 1) * FFN_SUB_ROWS) for i in range(n_sub)]
    ple_dot = lambda r: _dot(_bf16(p_ref[r, :]), w_ple_ref[...])
    ple = [ple_dot(r) for r in subs[:n_sub // 2]]
    xs = [x_ref[r, :] + _dot(y_ref[r, :], w_out_ref[...]) for r in subs]
    hn = [_bf16(x * _rms_scale(x) * g_ffn_ref[...]) for x in xs]
    hf = [_bf16(jnp.square(jnp.maximum(_dot(h, w1_ref[...]), 0.0))) for h in hn]
    xs = [x + _dot(h, w2_ref[...]) for x, h in zip(xs, hf)]
    ple += [ple_dot(r) for r in subs[n_sub // 2:-1]]
    gn = [_bf16(x * _rms_scale(x) * g_ple_ref[...]) for x in xs]
    gate = [jax.nn.sigmoid(_dot(g, w_gate_ref[...])) for g in gn]
    ple.append(ple_dot(subs[-1]))
    for r, x, g, e in zip(subs, xs, gate, ple):
        x = x + g * e
        o_ref[r, :] = x * _rms_scale(x) * g_final_ref[...]


def _const_spec(shape):
    zeros = (0,) * len(shape)
    return pl.BlockSpec(shape, lambda *_: zeros, pipeline_mode=pl.Buffered(1))


def _retention_tables(seq, rb):
    pos = np.arange(seq, dtype=np.float64)
    freqs = ROPE_BASE ** (-np.arange(0, HEAD_DIM, 2, dtype=np.float64) / HEAD_DIM)
    ang = pos[:, None] * freqs[None, :]
    cos = np.concatenate([np.cos(ang), np.cos(ang)], axis=1)
    sin = np.concatenate([-np.sin(ang), np.sin(ang)], axis=1)
    log_g = np.log(np.asarray(_GAMMAS, np.float64))
    t = np.arange(rb, dtype=np.float64)
    dist = np.abs(t[:, None] - t[None, :])
    visible = (t[None, :] // CHUNK) <= (t[:, None] // CHUNK)
    dmat = np.where(visible[None], np.exp(dist[None] * log_g[:, None, None]), 0.0)
    qdec = np.exp((t[None, :] + 1.0) * log_g[:, None])
    kdec = np.exp((rb - 1.0 - t)[None, :] * log_g[:, None])
    bcast = lambda a: np.broadcast_to(a[:, :, None], (RET_HEADS, rb, HEAD_DIM))
    return tuple(jnp.asarray(a, jnp.float32) for a in (cos, sin, dmat, bcast(qdec), bcast(kdec)))


def kernel(x, p, g_mix, w_in, ret_norm_g, sg_norm_g, w_s, b_s, w_out, g_ffn, w_ff1, w_ff2,
           g_ple, w_ple_gate, w_ple, g_final):
    batch, seq, d = x.shape
    ts, tm, rb = SEQ_TILE, ROW_TILE, RET_BLOCK
    assert w_in.shape[0] == 1, "one layer: the final RMSNorm is fused into its channel mixer"
    assert d == D_MODEL and seq % ts == 0 and (batch * seq) % tm == 0
    cos, sin, dmat, qdec, kdec = _retention_tables(seq, rb)
    tiles_per_row = seq // ts
    assert tiles_per_row % 2 == 0, "a grid step mixes a pair of seq tiles of one batch row"
    pairs_per_row = tiles_per_row // 2
    n_tiles = batch * tiles_per_row

    def x_tile_spec(tile_of_step):
        def index(k):
            t = jnp.minimum(tile_of_step(k), n_tiles - 1)
            return (t // tiles_per_row, t % tiles_per_row, 0)
        return pl.BlockSpec((1, ts, d), index)

    tab_spec = pl.BlockSpec((2 * ts, HEAD_DIM), lambda k: (k % pairs_per_row, 0))
    pair_spec = pl.BlockSpec((1, 2 * ts, d), lambda k: (k // pairs_per_row, k % pairs_per_row, 0))
    z_slabs = pltpu.VMEM((IN_WIDTH // LANES, ts, LANES), jnp.float32)
    n_steps = n_tiles // 2

    def row_chunk(w):
        rows, cols = w.shape
        assert rows % (n_steps * BF16_SUBLANES) == 0, "a bf16 block needs whole packed sublane tiles"
        return pl.BlockSpec((rows // n_steps, cols), lambda k: (k, 0))

    ffn_weights = (w_ff1[0], w_ff2[0], w_ple_gate[0])

    y, w_ff1_bf16, w_ff2_bf16, w_gate_bf16 = pl.pallas_call(
        functools.partial(_mixer_kernel, pairs_per_row=pairs_per_row),
        grid=(n_steps,),
        in_specs=[pl.BlockSpec((1, ts, d), lambda k: (0, 0, 0), pipeline_mode=pl.Buffered(1)),
                  x_tile_spec(lambda k: 2 * k + 1), x_tile_spec(lambda k: 2 * k + 2),
                  _const_spec((1, d)), _const_spec((d, IN_WIDTH)),
                  tab_spec, tab_spec,
                  _const_spec((RET_HEADS, rb, rb)),
                  _const_spec((RET_HEADS, rb, HEAD_DIM)),
                  _const_spec((RET_HEADS, rb, HEAD_DIM)),
                  _const_spec((1, RET_WIDTH)), _const_spec((1, SG_WIDTH)),
                  _const_spec((SG_GROUPS, SG_BLOCK, SG_BLOCK)),
                  _const_spec((SG_BLOCK, SG_GROUPS))] + [row_chunk(w) for w in ffn_weights],
        out_specs=[pair_spec] + [row_chunk(w) for w in ffn_weights],
        out_shape=[jax.ShapeDtypeStruct((batch, seq, d), jnp.bfloat16)]
        + [jax.ShapeDtypeStruct(w.shape, jnp.bfloat16) for w in ffn_weights],
        scratch_shapes=[pltpu.VMEM((RET_HEADS, HEAD_DIM, HEAD_DIM), jnp.float32),
                        z_slabs, z_slabs],
        compiler_params=pltpu.CompilerParams(
            dimension_semantics=("arbitrary",),
            vmem_limit_bytes=V7X_VMEM_LIMIT_BYTES),
        name="token_mixer",
    )(x, x, x, g_mix[0][None], _bf16(w_in[0]), cos, sin, dmat, qdec, kdec,
      ret_norm_g[0][None], sg_norm_g[0][None], w_s[0], b_s[0].T, *ffn_weights)

    row_spec = lambda w: pl.BlockSpec((tm, w), lambda i: (i, 0))
    x = pl.pallas_call(
        _ffn_kernel,
        grid=(batch * seq // tm,),
        in_specs=[row_spec(d), row_spec(d), row_spec(PLE_DIM), _const_spec((d, d)),
                  _const_spec((1, d)), _const_spec((d, D_FF)), _const_spec((D_FF, d)), _const_spec((1, d)),
                  _const_spec((d, d)), _const_spec((PLE_DIM, d)), _const_spec((1, d))],
        out_specs=row_spec(d),
        out_shape=jax.ShapeDtypeStruct((batch * seq, d), jnp.float32),
        compiler_params=pltpu.CompilerParams(
            dimension_semantics=("arbitrary",),
            vmem_limit_bytes=V7X_FFN_VMEM_LIMIT_BYTES),
        name="channel_mixer",
    )(x.reshape(batch * seq, d), y.reshape(batch * seq, d), p[0].reshape(batch * seq, PLE_DIM),
      _bf16(w_out[0]), g_ffn[0][None],
      w_ff1_bf16, w_ff2_bf16, g_ple[0][None], w_gate_bf16, _bf16(w_ple[0]), g_final[None])
    return x.reshape(batch, seq, d)
```

```python
import functools

import jax
import jax.numpy as jnp
import numpy as np
from jax import lax
from jax.experimental import pallas as pl
from jax.experimental.pallas import tpu as pltpu

D_MODEL = 1024
CHUNK = 64
RET_HEADS = 4
HEAD_DIM = 128
RET_WIDTH = RET_HEADS * HEAD_DIM
SG_BLOCK = 128
SG_GROUPS = 4
SG_DIM = 128
SG_WIDTH = SG_GROUPS * SG_DIM
IN_WIDTH = 4 * RET_WIDTH + 2 * SG_WIDTH
D_FF = 4 * D_MODEL
PLE_DIM = 256
ROPE_BASE = 10000.0
EPS = 1e-6

LANES = 128
BF16_SUBLANES = 16
SEQ_TILE = 512
RET_BLOCK = 256
ROW_TILE = 1024
FFN_SUB_ROWS = 256
PROJ_PARTS = 6
V7X_VMEM_LIMIT_BYTES = 56 * 1024 * 1024
V7X_FFN_VMEM_LIMIT_BYTES = 62 * 1024 * 1024

_GAMMAS = tuple(1.0 - 2.0 ** (-5.0 - h) for h in range(RET_HEADS))


def _bf16(a):
    return a.astype(jnp.bfloat16)


def _dot(a, b):
    return jnp.dot(a, b, preferred_element_type=jnp.float32)


def _rms_scale(x):
    return lax.rsqrt(jnp.mean(x * x, axis=-1, keepdims=True) + EPS)


def _layernorm_nogain(x):
    mu = jnp.mean(x, axis=-1, keepdims=True)
    xc = x - mu
    return xc * lax.rsqrt(jnp.mean(xc * xc, axis=-1, keepdims=True) + EPS)


def _project_parts(x_ref, g_mix_ref, w_in_ref, z_ref):
    x = x_ref[0]
    h = _bf16(x * _rms_scale(x) * g_mix_ref[...])
    width = IN_WIDTH // PROJ_PARTS

    def part(c):
        zc = _dot(h, w_in_ref[:, c * width:(c + 1) * width])
        for j in range(width // LANES):
            z_ref[c * (width // LANES) + j] = zc[:, j * LANES:(j + 1) * LANES]

    return [functools.partial(part, c) for c in range(PROJ_PARTS)]


def _mix_parts(z_ref, rows, cos_ref, sin_ref, dmat_ref, qdec_ref, kdec_ref,
               ret_g_ref, sg_g_ref, w_s_ref, b_s_ref, state_ref, o_ref):
    ts = z_ref.shape[1]
    y_parts = []
    slab = lambda part, idx: z_ref[part * RET_HEADS + idx]
    wide = lambda part: jnp.concatenate([slab(part, g) for g in range(SG_GROUPS)], axis=1)

    def head(hd):
        cos, sin = cos_ref[rows, :], sin_ref[rows, :]
        qh, kh, vh, gh = slab(0, hd), slab(1, hd), slab(2, hd), slab(3, hd)
        qr = qh * cos + pltpu.roll(qh, HEAD_DIM // 2, 1) * sin
        kr = (kh * cos + pltpu.roll(kh, HEAD_DIM // 2, 1) * sin) * HEAD_DIM ** -0.5
        vb = _bf16(vh)
        o_blocks = []
        for blk in range(ts // RET_BLOCK):
            brows = slice(blk * RET_BLOCK, (blk + 1) * RET_BLOCK)
            qb, kb, vbb = qr[brows], kr[brows], vb[brows]
            scores = lax.dot_general(_bf16(qb), _bf16(kb), (((1,), (1,)), ((), ())),
                                     preferred_element_type=jnp.float32)
            o = _dot(_bf16(scores * dmat_ref[hd]), vbb)
            o_blocks.append(o + _dot(_bf16(qb * qdec_ref[hd]), _bf16(state_ref[hd])))
            kv = lax.dot_general(_bf16(kb * kdec_ref[hd]), vbb, (((0,), (0,)), ((), ())),
                                 preferred_element_type=jnp.float32)
            state_ref[hd] = state_ref[hd] * (_GAMMAS[hd] ** RET_BLOCK) + kv
        o = jnp.concatenate(o_blocks, axis=0)
        r = _layernorm_nogain(o) * ret_g_ref[:, hd * HEAD_DIM:(hd + 1) * HEAD_DIM]
        y_parts.append(jax.nn.silu(gh) * r)

    def gating():
        sv = jax.nn.gelu(wide(5))
        svn = _bf16(_layernorm_nogain(sv) * sg_g_ref[...])
        row = lax.broadcasted_iota(jnp.int32, (SG_BLOCK, SG_BLOCK), 0)
        col = lax.broadcasted_iota(jnp.int32, (SG_BLOCK, SG_BLOCK), 1)
        allowed = (col // CHUNK) <= (row // CHUNK)
        n_blk = ts // SG_BLOCK
        for g in range(SG_GROUPS):
            w_g = _bf16(jnp.where(allowed, w_s_ref[g], 0.0))
            sl = slice(g * SG_DIM, (g + 1) * SG_DIM)
            vcat = jnp.concatenate(
                [svn[b * SG_BLOCK:(b + 1) * SG_BLOCK, sl] for b in range(n_blk)], axis=1)
            s_cat = _dot(w_g, vcat) + b_s_ref[:, g:g + 1]
            y_parts.append(jnp.concatenate(
                [s_cat[:, b * SG_DIM:(b + 1) * SG_DIM] for b in range(n_blk)], axis=0))

    def store():
        for g in range(SG_GROUPS):
            y_parts[RET_HEADS + g] = jax.nn.gelu(slab(4, g)) * y_parts[RET_HEADS + g]
        o_ref[0, rows, :] = _bf16(jnp.concatenate(y_parts, axis=1))

    return [functools.partial(head, hd) for hd in range(RET_HEADS)] + [gating, store]


def _mixer_kernel(x_first_ref, x_odd_ref, x_even_ref, g_mix_ref, w_in_ref,
                  cos_ref, sin_ref, dmat_ref, qdec_ref, kdec_ref,
                  ret_g_ref, sg_g_ref, w_s_ref, b_s_ref,
                  w_ff1_ref, w_ff2_ref, w_gate_ref,
                  o_ref, w_ff1_bf16_ref, w_ff2_bf16_ref, w_gate_bf16_ref,
                  state_ref, z_even_ref, z_odd_ref, *, pairs_per_row):
    w_ff1_bf16_ref[...] = _bf16(w_ff1_ref[...])
    w_ff2_bf16_ref[...] = _bf16(w_ff2_ref[...])
    w_gate_bf16_ref[...] = _bf16(w_gate_ref[...])

    ts = x_odd_ref.shape[1]
    k = pl.program_id(0)
    project = lambda x_ref, z_ref: _project_parts(x_ref, g_mix_ref, w_in_ref, z_ref)
    mix = lambda z_ref, rows: _mix_parts(
        z_ref, rows, cos_ref, sin_ref, dmat_ref, qdec_ref, kdec_ref,
        ret_g_ref, sg_g_ref, w_s_ref, b_s_ref, state_ref, o_ref)

    def interleave(proj_parts, mix_parts):
        for proj_part, mix_part in zip(proj_parts, mix_parts, strict=True):
            proj_part()
            mix_part()

    @pl.when(k == 0)
    def _():
        for proj_part in project(x_first_ref, z_even_ref):
            proj_part()

    @pl.when(k % pairs_per_row == 0)
    def _():
        state_ref[...] = jnp.zeros_like(state_ref)

    interleave(project(x_odd_ref, z_odd_ref), mix(z_even_ref, slice(0, ts)))
    interleave(project(x_even_ref, z_even_ref), mix(z_odd_ref, slice(ts, 2 * ts)))


def _ffn_kernel(x_ref, y_ref, p_ref, w_out_ref, g_ffn_ref, w1_ref, w2_ref, g_ple_ref, w_gate_ref,
                w_ple_ref, g_final_ref, o_ref):
    n_sub = x_ref.shape[0] // FFN_SUB_ROWS
    subs = [slice(i * FFN_SUB_ROWS, (i + 1) * FFN_SUB_ROWS) for i in range(n_sub)]
    xs = [x_ref[r, :] + _dot(y_ref[r, :], w_out_ref[...]) for r in subs]
    hn = [_bf16(x * _rms_scale(x) * g_ffn_ref[...]) for x in xs]
    hf = [_bf16(jnp.square(jnp.maximum(_dot(h, w1_ref[...]), 0.0))) for h in hn]
    xs = [x + _dot(h, w2_ref[...]) for x, h in zip(xs, hf)]
    ple = [_dot(_bf16(p_ref[r, :]), w_ple_ref[...]) for r in subs]
    gn = [_bf16(x * _rms_scale(x) * g_ple_ref[...]) for x in xs]
    for r, x, g, e in zip(subs, xs, gn, ple):
        x = x + jax.nn.sigmoid(_dot(g, w_gate_ref[...])) * e
        o_ref[r, :] = x * _rms_scale(x) * g_final_ref[...]


def _const_spec(shape):
    zeros = (0,) * len(shape)
    return pl.BlockSpec(shape, lambda *_: zeros, pipeline_mode=pl.Buffered(1))


def _retention_tables(seq, rb):
    pos = np.arange(seq, dtype=np.float64)
    freqs = ROPE_BASE ** (-np.arange(0, HEAD_DIM, 2, dtype=np.float64) / HEAD_DIM)
    ang = pos[:, None] * freqs[None, :]
    cos = np.concatenate([np.cos(ang), np.cos(ang)], axis=1)
    sin = np.concatenate([-np.sin(ang), np.sin(ang)], axis=1)
    log_g = np.log(np.asarray(_GAMMAS, np.float64))
    t = np.arange(rb, dtype=np.float64)
    dist = np.abs(t[:, None] - t[None, :])
    visible = (t[None, :] // CHUNK) <= (t[:, None] // CHUNK)
    dmat = np.where(visible[None], np.exp(dist[None] * log_g[:, None, None]), 0.0)
    qdec = np.exp((t[None, :] + 1.0) * log_g[:, None])
    kdec = np.exp((rb - 1.0 - t)[None, :] * log_g[:, None])
    bcast = lambda a: np.broadcast_to(a[:, :, None], (RET_HEADS, rb, HEAD_DIM))
    return tuple(jnp.asarray(a, jnp.float32) for a in (cos, sin, dmat, bcast(qdec), bcast(kdec)))


def kernel(x, p, g_mix, w_in, ret_norm_g, sg_norm_g, w_s, b_s, w_out, g_ffn, w_ff1, w_ff2,
           g_ple, w_ple_gate, w_ple, g_final):
    batch, seq, d = x.shape
    ts, tm, rb = SEQ_TILE, ROW_TILE, RET_BLOCK
    assert w_in.shape[0] == 1, "one layer: the final RMSNorm is fused into its channel mixer"
    assert d == D_MODEL and seq % ts == 0 and (batch * seq) % tm == 0
    cos, sin, dmat, qdec, kdec = _retention_tables(seq, rb)
    tiles_per_row = seq // ts
    assert tiles_per_row % 2 == 0, "a grid step mixes a pair of seq tiles of one batch row"
    pairs_per_row = tiles_per_row // 2
    n_tiles = batch * tiles_per_row

    def x_tile_spec(tile_of_step):
        def index(k):
            t = jnp.minimum(tile_of_step(k), n_tiles - 1)
            return (t // tiles_per_row, t % tiles_per_row, 0)
        return pl.BlockSpec((1, ts, d), index)

    tab_spec = pl.BlockSpec((2 * ts, HEAD_DIM), lambda k: (k % pairs_per_row, 0))
    pair_spec = pl.BlockSpec((1, 2 * ts, d), lambda k: (k // pairs_per_row, k % pairs_per_row, 0))
    z_slabs = pltpu.VMEM((IN_WIDTH // LANES, ts, LANES), jnp.float32)
    n_steps = n_tiles // 2

    def row_chunk(w):
        rows, cols = w.shape
        assert rows % (n_steps * BF16_SUBLANES) == 0, "a bf16 block needs whole packed sublane tiles"
        return pl.BlockSpec((rows // n_steps, cols), lambda k: (k, 0))

    ffn_weights = (w_ff1[0], w_ff2[0], w_ple_gate[0])

    y, w_ff1_bf16, w_ff2_bf16, w_gate_bf16 = pl.pallas_call(
        functools.partial(_mixer_kernel, pairs_per_row=pairs_per_row),
        grid=(n_steps,),
        in_specs=[pl.BlockSpec((1, ts, d), lambda k: (0, 0, 0), pipeline_mode=pl.Buffered(1)),
                  x_tile_spec(lambda k: 2 * k + 1), x_tile_spec(lambda k: 2 * k + 2),
                  _const_spec((1, d)), _const_spec((d, IN_WIDTH)),
                  tab_spec, tab_spec,
                  _const_spec((RET_HEADS, rb, rb)),
                  _const_spec((RET_HEADS, rb, HEAD_DIM)),
                  _const_spec((RET_HEADS, rb, HEAD_DIM)),
                  _const_spec((1, RET_WIDTH)), _const_spec((1, SG_WIDTH)),
                  _const_spec((SG_GROUPS, SG_BLOCK, SG_BLOCK)),
                  _const_spec((SG_BLOCK, SG_GROUPS))] + [row_chunk(w) for w in ffn_weights],
        out_specs=[pair_spec] + [row_chunk(w) for w in ffn_weights],
        out_shape=[jax.ShapeDtypeStruct((batch, seq, d), jnp.bfloat16)]
        + [jax.ShapeDtypeStruct(w.shape, jnp.bfloat16) for w in ffn_weights],
        scratch_shapes=[pltpu.VMEM((RET_HEADS, HEAD_DIM, HEAD_DIM), jnp.float32),
                        z_slabs, z_slabs],
        compiler_params=pltpu.CompilerParams(
            dimension_semantics=("arbitrary",),
            vmem_limit_bytes=V7X_VMEM_LIMIT_BYTES),
        name="token_mixer",
    )(x, x, x, g_mix[0][None], _bf16(w_in[0]), cos, sin, dmat, qdec, kdec,
      ret_norm_g[0][None], sg_norm_g[0][None], w_s[0], b_s[0].T, *ffn_weights)

    row_spec = lambda w: pl.BlockSpec((tm, w), lambda i: (i, 0))
    x = pl.pallas_call(
        _ffn_kernel,
        grid=(batch * seq // tm,),
        in_specs=[row_spec(d), row_spec(d), row_spec(PLE_DIM), _const_spec((d, d)),
                  _const_spec((1, d)), _const_spec((d, D_FF)), _const_spec((D_FF, d)), _const_spec((1, d)),
                  _const_spec((d, d)), _const_spec((PLE_DIM, d)), _const_spec((1, d))],
        out_specs=row_spec(d),
        out_shape=jax.ShapeDtypeStruct((batch * seq, d), jnp.float32),
        compiler_params=pltpu.CompilerParams(
            dimension_semantics=("arbitrary",),
            vmem_limit_bytes=V7X_FFN_VMEM_LIMIT_BYTES),
        name="channel_mixer",
    )(x.reshape(batch * seq, d), y.reshape(batch * seq, d), p[0].reshape(batch * seq, PLE_DIM),
      _bf16(w_out[0]), g_ffn[0][None],
      w_ff1_bf16, w_ff2_bf16, g_ple[0][None], w_gate_bf16, _bf16(w_ple[0]), g_final[None])
    return x.reshape(batch, seq, d)
```

```python
import functools

import jax
import jax.numpy as jnp
import numpy as np
from jax import lax
from jax.experimental import pallas as pl
from jax.experimental.pallas import tpu as pltpu

D_MODEL = 1024
CHUNK = 64
RET_HEADS = 4
HEAD_DIM = 128
RET_WIDTH = RET_HEADS * HEAD_DIM
SG_BLOCK = 128
SG_GROUPS = 4
SG_DIM = 128
SG_WIDTH = SG_GROUPS * SG_DIM
IN_WIDTH = 4 * RET_WIDTH + 2 * SG_WIDTH
D_FF = 4 * D_MODEL
PLE_DIM = 256
ROPE_BASE = 10000.0
EPS = 1e-6

LANES = 128
BF16_SUBLANES = 16
SEQ_TILE = 512
RET_BLOCK = 256
ROW_TILE = 1024
FFN_SUB_ROWS = 256
OUT_ROW_TILE = 2048
OUT_SUB_ROWS = 512
PROJ_PARTS = 6
V7X_VMEM_LIMIT_BYTES = 56 * 1024 * 1024

_GAMMAS = tuple(1.0 - 2.0 ** (-5.0 - h) for h in range(RET_HEADS))


def _bf16(a):
    return a.astype(jnp.bfloat16)


def _dot(a, b):
    return jnp.dot(a, b, preferred_element_type=jnp.float32)


def _rms_scale(x):
    return lax.rsqrt(jnp.mean(x * x, axis=-1, keepdims=True) + EPS)


def _layernorm_nogain(x):
    mu = jnp.mean(x, axis=-1, keepdims=True)
    xc = x - mu
    return xc * lax.rsqrt(jnp.mean(xc * xc, axis=-1, keepdims=True) + EPS)


def _project_parts(x_ref, g_mix_ref, w_in_ref, z_ref):
    x = x_ref[0]
    h = _bf16(x * _rms_scale(x) * g_mix_ref[...])
    width = IN_WIDTH // PROJ_PARTS

    def part(c):
        zc = _dot(h, w_in_ref[:, c * width:(c + 1) * width])
        for j in range(width // LANES):
            z_ref[c * (width // LANES) + j] = zc[:, j * LANES:(j + 1) * LANES]

    return [functools.partial(part, c) for c in range(PROJ_PARTS)]


def _mix_parts(z_ref, rows, cos_ref, sin_ref, dmat_ref, qdec_ref, kdec_ref,
               ret_g_ref, sg_g_ref, w_s_ref, b_s_ref, state_ref, o_ref):
    ts = z_ref.shape[1]
    y_parts = []
    slab = lambda part, idx: z_ref[part * RET_HEADS + idx]
    wide = lambda part: jnp.concatenate([slab(part, g) for g in range(SG_GROUPS)], axis=1)

    def head(hd):
        cos, sin = cos_ref[rows, :], sin_ref[rows, :]
        qh, kh, vh, gh = slab(0, hd), slab(1, hd), slab(2, hd), slab(3, hd)
        qr = qh * cos + pltpu.roll(qh, HEAD_DIM // 2, 1) * sin
        kr = (kh * cos + pltpu.roll(kh, HEAD_DIM // 2, 1) * sin) * HEAD_DIM ** -0.5
        vb = _bf16(vh)
        o_blocks = []
        for blk in range(ts // RET_BLOCK):
            brows = slice(blk * RET_BLOCK, (blk + 1) * RET_BLOCK)
            qb, kb, vbb = qr[brows], kr[brows], vb[brows]
            scores = lax.dot_general(_bf16(qb), _bf16(kb), (((1,), (1,)), ((), ())),
                                     preferred_element_type=jnp.float32)
            o = _dot(_bf16(scores * dmat_ref[hd]), vbb)
            o_blocks.append(o + _dot(_bf16(qb * qdec_ref[hd]), _bf16(state_ref[hd])))
            kv = lax.dot_general(_bf16(kb * kdec_ref[hd]), vbb, (((0,), (0,)), ((), ())),
                                 preferred_element_type=jnp.float32)
            state_ref[hd] = state_ref[hd] * (_GAMMAS[hd] ** RET_BLOCK) + kv
        o = jnp.concatenate(o_blocks, axis=0)
        r = _layernorm_nogain(o) * ret_g_ref[:, hd * HEAD_DIM:(hd + 1) * HEAD_DIM]
        y_parts.append(jax.nn.silu(gh) * r)

    def gating():
        sv = jax.nn.gelu(wide(5))
        svn = _bf16(_layernorm_nogain(sv) * sg_g_ref[...])
        row = lax.broadcasted_iota(jnp.int32, (SG_BLOCK, SG_BLOCK), 0)
        col = lax.broadcasted_iota(jnp.int32, (SG_BLOCK, SG_BLOCK), 1)
        allowed = (col // CHUNK) <= (row // CHUNK)
        n_blk = ts // SG_BLOCK
        for g in range(SG_GROUPS):
            w_g = _bf16(jnp.where(allowed, w_s_ref[g], 0.0))
            sl = slice(g * SG_DIM, (g + 1) * SG_DIM)
            vcat = jnp.concatenate(
                [svn[b * SG_BLOCK:(b + 1) * SG_BLOCK, sl] for b in range(n_blk)], axis=1)
            s_cat = _dot(w_g, vcat) + b_s_ref[:, g:g + 1]
            y_parts.append(jnp.concatenate(
                [s_cat[:, b * SG_DIM:(b + 1) * SG_DIM] for b in range(n_blk)], axis=0))

    def store():
        for g in range(SG_GROUPS):
            y_parts[RET_HEADS + g] = jax.nn.gelu(slab(4, g)) * y_parts[RET_HEADS + g]
        o_ref[0, rows, :] = _bf16(jnp.concatenate(y_parts, axis=1))

    return [functools.partial(head, hd) for hd in range(RET_HEADS)] + [gating, store]


def _mixer_kernel(x_first_ref, x_odd_ref, x_even_ref, g_mix_ref, w_in_ref,
                  cos_ref, sin_ref, dmat_ref, qdec_ref, kdec_ref,
                  ret_g_ref, sg_g_ref, w_s_ref, b_s_ref,
                  w_ff1_ref, w_ff2_ref, w_gate_ref,
                  o_ref, w_ff1_bf16_ref, w_ff2_bf16_ref, w_gate_bf16_ref,
                  state_ref, z_even_ref, z_odd_ref, *, pairs_per_row):
    w_ff1_bf16_ref[...] = _bf16(w_ff1_ref[...])
    w_ff2_bf16_ref[...] = _bf16(w_ff2_ref[...])
    w_gate_bf16_ref[...] = _bf16(w_gate_ref[...])

    ts = x_odd_ref.shape[1]
    k = pl.program_id(0)
    project = lambda x_ref, z_ref: _project_parts(x_ref, g_mix_ref, w_in_ref, z_ref)
    mix = lambda z_ref, rows: _mix_parts(
        z_ref, rows, cos_ref, sin_ref, dmat_ref, qdec_ref, kdec_ref,
        ret_g_ref, sg_g_ref, w_s_ref, b_s_ref, state_ref, o_ref)

    def interleave(proj_parts, mix_parts):
        for proj_part, mix_part in zip(proj_parts, mix_parts, strict=True):
            proj_part()
            mix_part()

    @pl.when(k == 0)
    def _():
        for proj_part in project(x_first_ref, z_even_ref):
            proj_part()

    @pl.when(k % pairs_per_row == 0)
    def _():
        state_ref[...] = jnp.zeros_like(state_ref)

    interleave(project(x_odd_ref, z_odd_ref), mix(z_even_ref, slice(0, ts)))
    interleave(project(x_even_ref, z_even_ref), mix(z_odd_ref, slice(ts, 2 * ts)))


def _out_proj_kernel(x_ref, y_ref, w_out_ref, o_ref):
    for i in range(x_ref.shape[0] // OUT_SUB_ROWS):
        r = slice(i * OUT_SUB_ROWS, (i + 1) * OUT_SUB_ROWS)
        o_ref[r, :] = x_ref[r, :] + _dot(y_ref[r, :], w_out_ref[...])


def _ffn_kernel(x_ref, p_ref, g_ffn_ref, w1_ref, w2_ref, g_ple_ref, w_gate_ref, w_ple_ref,
                g_final_ref, o_ref):
    n_sub = x_ref.shape[0] // FFN_SUB_ROWS
    subs = [slice(i * FFN_SUB_ROWS, (i + 1) * FFN_SUB_ROWS) for i in range(n_sub)]
    ple_dot = lambda r: _dot(_bf16(p_ref[r, :]), w_ple_ref[...])
    ple = [ple_dot(r) for r in subs[:n_sub // 2]]
    xs = [x_ref[r, :] for r in subs]
    hn = [_bf16(x * _rms_scale(x) * g_ffn_ref[...]) for x in xs]
    hf = [jnp.maximum(_dot(h, w1_ref[...]), 0.0) for h in hn]
    xs = [x + _dot(_bf16(h * h), w2_ref[...]) for x, h in zip(xs, hf)]
    ple += [ple_dot(r) for r in subs[n_sub // 2:-1]]
    gn = [_bf16(x * _rms_scale(x) * g_ple_ref[...]) for x in xs]
    gate = [jax.nn.sigmoid(_dot(g, w_gate_ref[...])) for g in gn]
    ple.append(ple_dot(subs[-1]))
    for r, x, g, e in zip(subs, xs, gate, ple):
        x = x + g * e
        o_ref[r, :] = x * _rms_scale(x) * g_final_ref[...]


def _const_spec(shape):
    zeros = (0,) * len(shape)
    return pl.BlockSpec(shape, lambda *_: zeros, pipeline_mode=pl.Buffered(1))


def _retention_tables(seq, rb):
    pos = np.arange(seq, dtype=np.float64)
    freqs = ROPE_BASE ** (-np.arange(0, HEAD_DIM, 2, dtype=np.float64) / HEAD_DIM)
    ang = pos[:, None] * freqs[None, :]
    cos = np.concatenate([np.cos(ang), np.cos(ang)], axis=1)
    sin = np.concatenate([-np.sin(ang), np.sin(ang)], axis=1)
    log_g = np.log(np.asarray(_GAMMAS, np.float64))
    t = np.arange(rb, dtype=np.float64)
    dist = np.abs(t[:, None] - t[None, :])
    visible = (t[None, :] // CHUNK) <= (t[:, None] // CHUNK)
    dmat = np.where(visible[None], np.exp(dist[None] * log_g[:, None, None]), 0.0)
    qdec = np.exp((t[None, :] + 1.0) * log_g[:, None])
    kdec = np.exp((rb - 1.0 - t)[None, :] * log_g[:, None])
    bcast = lambda a: np.broadcast_to(a[:, :, None], (RET_HEADS, rb, HEAD_DIM))
    return tuple(jnp.asarray(a, jnp.float32) for a in (cos, sin, dmat, bcast(qdec), bcast(kdec)))


def kernel(x, p, g_mix, w_in, ret_norm_g, sg_norm_g, w_s, b_s, w_out, g_ffn, w_ff1, w_ff2,
           g_ple, w_ple_gate, w_ple, g_final):
    batch, seq, d = x.shape
    ts, tm, rb = SEQ_TILE, ROW_TILE, RET_BLOCK
    assert w_in.shape[0] == 1, "one layer: the final RMSNorm is fused into its channel mixer"
    assert d == D_MODEL and seq % ts == 0 and (batch * seq) % tm == 0
    cos, sin, dmat, qdec, kdec = _retention_tables(seq, rb)
    tiles_per_row = seq // ts
    assert tiles_per_row % 2 == 0, "a grid step mixes a pair of seq tiles of one batch row"
    pairs_per_row = tiles_per_row // 2
    n_tiles = batch * tiles_per_row

    def x_tile_spec(tile_of_step):
        def index(k):
            t = jnp.minimum(tile_of_step(k), n_tiles - 1)
            return (t // tiles_per_row, t % tiles_per_row, 0)
        return pl.BlockSpec((1, ts, d), index)

    tab_spec = pl.BlockSpec((2 * ts, HEAD_DIM), lambda k: (k % pairs_per_row, 0))
    pair_spec = pl.BlockSpec((1, 2 * ts, d), lambda k: (k // pairs_per_row, k % pairs_per_row, 0))
    z_slabs = pltpu.VMEM((IN_WIDTH // LANES, ts, LANES), jnp.float32)
    n_steps = n_tiles // 2

    def row_chunk(w):
        rows, cols = w.shape
        assert rows % (n_steps * BF16_SUBLANES) == 0, "a bf16 block needs whole packed sublane tiles"
        return pl.BlockSpec((rows // n_steps, cols), lambda k: (k, 0))

    ffn_weights = (w_ff1[0], w_ff2[0], w_ple_gate[0])

    y, w_ff1_bf16, w_ff2_bf16, w_gate_bf16 = pl.pallas_call(
        functools.partial(_mixer_kernel, pairs_per_row=pairs_per_row),
        grid=(n_steps,),
        in_specs=[pl.BlockSpec((1, ts, d), lambda k: (0, 0, 0), pipeline_mode=pl.Buffered(1)),
                  x_tile_spec(lambda k: 2 * k + 1), x_tile_spec(lambda k: 2 * k + 2),
                  _const_spec((1, d)), _const_spec((d, IN_WIDTH)),
                  tab_spec, tab_spec,
                  _const_spec((RET_HEADS, rb, rb)),
                  _const_spec((RET_HEADS, rb, HEAD_DIM)),
                  _const_spec((RET_HEADS, rb, HEAD_DIM)),
                  _const_spec((1, RET_WIDTH)), _const_spec((1, SG_WIDTH)),
                  _const_spec((SG_GROUPS, SG_BLOCK, SG_BLOCK)),
                  _const_spec((SG_BLOCK, SG_GROUPS))] + [row_chunk(w) for w in ffn_weights],
        out_specs=[pair_spec] + [row_chunk(w) for w in ffn_weights],
        out_shape=[jax.ShapeDtypeStruct((batch, seq, d), jnp.bfloat16)]
        + [jax.ShapeDtypeStruct(w.shape, jnp.bfloat16) for w in ffn_weights],
        scratch_shapes=[pltpu.VMEM((RET_HEADS, HEAD_DIM, HEAD_DIM), jnp.float32),
                        z_slabs, z_slabs],
        compiler_params=pltpu.CompilerParams(
            dimension_semantics=("arbitrary",),
            vmem_limit_bytes=V7X_VMEM_LIMIT_BYTES),
        name="token_mixer",
    )(x, x, x, g_mix[0][None], _bf16(w_in[0]), cos, sin, dmat, qdec, kdec,
      ret_norm_g[0][None], sg_norm_g[0][None], w_s[0], b_s[0].T, *ffn_weights)

    rows_spec = lambda rows, w: pl.BlockSpec((rows, w), lambda i: (i, 0))
    x = pl.pallas_call(
        _out_proj_kernel,
        grid=(batch * seq // OUT_ROW_TILE,),
        in_specs=[rows_spec(OUT_ROW_TILE, d), rows_spec(OUT_ROW_TILE, d), _const_spec((d, d))],
        out_specs=rows_spec(OUT_ROW_TILE, d),
        out_shape=jax.ShapeDtypeStruct((batch * seq, d), jnp.float32),
        compiler_params=pltpu.CompilerParams(
            dimension_semantics=("arbitrary",),
            vmem_limit_bytes=V7X_VMEM_LIMIT_BYTES),
        name="out_proj",
    )(x.reshape(batch * seq, d), y.reshape(batch * seq, d), _bf16(w_out[0]))

    row_spec = functools.partial(rows_spec, tm)
    x = pl.pallas_call(
        _ffn_kernel,
        grid=(batch * seq // tm,),
        in_specs=[row_spec(d), row_spec(PLE_DIM), _const_spec((1, d)),
                  _const_spec((d, D_FF)), _const_spec((D_FF, d)), _const_spec((1, d)),
                  _const_spec((d, d)), _const_spec((PLE_DIM, d)), _const_spec((1, d))],
        out_specs=row_spec(d),
        out_shape=jax.ShapeDtypeStruct((batch * seq, d), jnp.float32),
        compiler_params=pltpu.CompilerParams(
            dimension_semantics=("arbitrary",),
            vmem_limit_bytes=V7X_VMEM_LIMIT_BYTES),
        name="channel_mixer",
    )(x.reshape(batch * seq, d), p[0].reshape(batch * seq, PLE_DIM), g_ffn[0][None],
      w_ff1_bf16, w_ff2_bf16, g_ple[0][None], w_gate_bf16, _bf16(w_ple[0]), g_final[None])
    return x.reshape(batch, seq, d)
```

```python
import functools

import jax
import jax.numpy as jnp
import numpy as np
from jax import lax
from jax.experimental import pallas as pl
from jax.experimental.pallas import tpu as pltpu

D_MODEL = 1024
CHUNK = 64
RET_HEADS = 4
HEAD_DIM = 128
RET_WIDTH = RET_HEADS * HEAD_DIM
SG_BLOCK = 128
SG_GROUPS = 4
SG_DIM = 128
SG_WIDTH = SG_GROUPS * SG_DIM
IN_WIDTH = 4 * RET_WIDTH + 2 * SG_WIDTH
D_FF = 4 * D_MODEL
PLE_DIM = 256
ROPE_BASE = 10000.0
EPS = 1e-6

LANES = 128
BF16_SUBLANES = 16
SEQ_TILE = 512
RET_BLOCK = 256
ROW_TILE = 1024
FFN_SUB_ROWS = 256
PROJ_PARTS = 6
V7X_VMEM_LIMIT_BYTES = 56 * 1024 * 1024

_GAMMAS = tuple(1.0 - 2.0 ** (-5.0 - h) for h in range(RET_HEADS))


def _bf16(a):
    return a.astype(jnp.bfloat16)


def _dot(a, b):
    return jnp.dot(a, b, preferred_element_type=jnp.float32)


def _rms_scale(x):
    return lax.rsqrt(jnp.mean(x * x, axis=-1, keepdims=True) + EPS)


def _layernorm_nogain(x):
    mu = jnp.mean(x, axis=-1, keepdims=True)
    xc = x - mu
    return xc * lax.rsqrt(jnp.mean(xc * xc, axis=-1, keepdims=True) + EPS)


def _project_parts(x_ref, g_mix_ref, w_in_ref, z_ref, xkeep_ref):
    x = x_ref[0]
    xkeep_ref[...] = x
    h = _bf16(x * _rms_scale(x) * g_mix_ref[...])
    width = IN_WIDTH // PROJ_PARTS

    def part(c):
        zc = _dot(h, w_in_ref[:, c * width:(c + 1) * width])
        for j in range(width // LANES):
            z_ref[c * (width // LANES) + j] = zc[:, j * LANES:(j + 1) * LANES]

    return [functools.partial(part, c) for c in range(PROJ_PARTS)]


def _mix_parts(z_ref, x_ref, rows, cos_ref, sin_ref, dmat_ref, qdec_ref, kdec_ref,
               ret_g_ref, sg_g_ref, w_s_ref, b_s_ref, w_out_ref, state_ref, o_ref):
    ts = z_ref.shape[1]
    y_parts = []
    slab = lambda part, idx: z_ref[part * RET_HEADS + idx]
    wide = lambda part: jnp.concatenate([slab(part, g) for g in range(SG_GROUPS)], axis=1)

    def head(hd):
        cos, sin = cos_ref[rows, :], sin_ref[rows, :]
        qh, kh, vh, gh = slab(0, hd), slab(1, hd), slab(2, hd), slab(3, hd)
        qr = qh * cos + pltpu.roll(qh, HEAD_DIM // 2, 1) * sin
        kr = (kh * cos + pltpu.roll(kh, HEAD_DIM // 2, 1) * sin) * HEAD_DIM ** -0.5
        vb = _bf16(vh)
        o_blocks = []
        for blk in range(ts // RET_BLOCK):
            brows = slice(blk * RET_BLOCK, (blk + 1) * RET_BLOCK)
            qb, kb, vbb = qr[brows], kr[brows], vb[brows]
            scores = lax.dot_general(_bf16(qb), _bf16(kb), (((1,), (1,)), ((), ())),
                                     preferred_element_type=jnp.float32)
            o = _dot(_bf16(scores * dmat_ref[hd]), vbb)
            o_blocks.append(o + _dot(_bf16(qb * qdec_ref[hd]), _bf16(state_ref[hd])))
            kv = lax.dot_general(_bf16(kb * kdec_ref[hd]), vbb, (((0,), (0,)), ((), ())),
                                 preferred_element_type=jnp.float32)
            state_ref[hd] = state_ref[hd] * (_GAMMAS[hd] ** RET_BLOCK) + kv
        o = jnp.concatenate(o_blocks, axis=0)
        r = _layernorm_nogain(o) * ret_g_ref[:, hd * HEAD_DIM:(hd + 1) * HEAD_DIM]
        y_parts.append(jax.nn.silu(gh) * r)

    def gating():
        sv = jax.nn.gelu(wide(5))
        svn = _bf16(_layernorm_nogain(sv) * sg_g_ref[...])
        row = lax.broadcasted_iota(jnp.int32, (SG_BLOCK, SG_BLOCK), 0)
        col = lax.broadcasted_iota(jnp.int32, (SG_BLOCK, SG_BLOCK), 1)
        allowed = (col // CHUNK) <= (row // CHUNK)
        n_blk = ts // SG_BLOCK
        for g in range(SG_GROUPS):
            w_g = _bf16(jnp.where(allowed, w_s_ref[g], 0.0))
            sl = slice(g * SG_DIM, (g + 1) * SG_DIM)
            vcat = jnp.concatenate(
                [svn[b * SG_BLOCK:(b + 1) * SG_BLOCK, sl] for b in range(n_blk)], axis=1)
            s_cat = _dot(w_g, vcat) + b_s_ref[:, g:g + 1]
            y_parts.append(jnp.concatenate(
                [s_cat[:, b * SG_DIM:(b + 1) * SG_DIM] for b in range(n_blk)], axis=0))

    def out_proj():
        for g in range(SG_GROUPS):
            y_parts[RET_HEADS + g] = jax.nn.gelu(slab(4, g)) * y_parts[RET_HEADS + g]
        y = _bf16(jnp.concatenate(y_parts, axis=1))
        o_ref[0, rows, :] = x_ref[...] + _dot(y, w_out_ref[...])

    return [functools.partial(head, hd) for hd in range(RET_HEADS)] + [gating, out_proj]


def _mixer_kernel(x_first_ref, x_odd_ref, x_even_ref, w_in_ref,
                  cos_ref, sin_ref, dmat_ref, qdec_ref, kdec_ref,
                  w_s_ref, b_s_ref, w_out_ref,
                  w_ff1_ref, w_ff2_ref, w_gate_ref, g_mix_ref, ret_g_ref, sg_g_ref,
                  o_ref, w_ff1_bf16_ref, w_ff2_bf16_ref, w_gate_bf16_ref,
                  state_ref, z_even_ref, z_odd_ref, xk_even_ref, xk_odd_ref, *, pairs_per_row):
    w_ff1_bf16_ref[...] = _bf16(w_ff1_ref[...])
    w_ff2_bf16_ref[...] = _bf16(w_ff2_ref[...])
    w_gate_bf16_ref[...] = _bf16(w_gate_ref[...])

    ts = x_odd_ref.shape[1]
    k = pl.program_id(0)
    project = lambda x_ref, z_ref, xk_ref: _project_parts(
        x_ref, g_mix_ref, w_in_ref, z_ref, xk_ref)
    mix = lambda z_ref, xk_ref, rows: _mix_parts(
        z_ref, xk_ref, rows, cos_ref, sin_ref, dmat_ref, qdec_ref, kdec_ref,
        ret_g_ref, sg_g_ref, w_s_ref, b_s_ref, w_out_ref, state_ref, o_ref)

    def interleave(proj_parts, mix_parts):
        for proj_part, mix_part in zip(proj_parts, mix_parts, strict=True):
            proj_part()
            mix_part()

    @pl.when(k == 0)
    def _():
        for proj_part in project(x_first_ref, z_even_ref, xk_even_ref):
            proj_part()

    @pl.when(k % pairs_per_row == 0)
    def _():
        state_ref[...] = jnp.zeros_like(state_ref)

    interleave(project(x_odd_ref, z_odd_ref, xk_odd_ref),
               mix(z_even_ref, xk_even_ref, slice(0, ts)))
    interleave(project(x_even_ref, z_even_ref, xk_even_ref),
               mix(z_odd_ref, xk_odd_ref, slice(ts, 2 * ts)))


def _ffn_kernel(x_ref, p_ref, g_ffn_ref, w1_ref, w2_ref, g_ple_ref, w_gate_ref, w_ple_ref,
                g_final_ref, o_ref):
    n_sub = x_ref.shape[0] // FFN_SUB_ROWS
    subs = [slice(i * FFN_SUB_ROWS, (i + 1) * FFN_SUB_ROWS) for i in range(n_sub)]
    ple_dot = lambda r: _dot(_bf16(p_ref[r, :]), w_ple_ref[...])
    ple = [ple_dot(r) for r in subs[:n_sub // 2]]
    xs = [x_ref[r, :] for r in subs]
    hn = [_bf16(x * _rms_scale(x) * g_ffn_ref[...]) for x in xs]
    hf = [jnp.maximum(_dot(h, w1_ref[...]), 0.0) for h in hn]
    xs = [x + _dot(_bf16(h * h), w2_ref[...]) for x, h in zip(xs, hf)]
    ple += [ple_dot(r) for r in subs[n_sub // 2:-1]]
    gn = [_bf16(x * _rms_scale(x) * g_ple_ref[...]) for x in xs]
    gate = [jax.nn.sigmoid(_dot(g, w_gate_ref[...])) for g in gn]
    ple.append(ple_dot(subs[-1]))
    for r, x, g, e in zip(subs, xs, gate, ple):
        x = x + g * e
        o_ref[r, :] = x * _rms_scale(x) * g_final_ref[...]


def _const_spec(shape):
    zeros = (0,) * len(shape)
    return pl.BlockSpec(shape, lambda *_: zeros, pipeline_mode=pl.Buffered(1))


def _retention_tables(seq, rb):
    pos = np.arange(seq, dtype=np.float64)
    freqs = ROPE_BASE ** (-np.arange(0, HEAD_DIM, 2, dtype=np.float64) / HEAD_DIM)
    ang = pos[:, None] * freqs[None, :]
    cos = np.concatenate([np.cos(ang), np.cos(ang)], axis=1)
    sin = np.concatenate([-np.sin(ang), np.sin(ang)], axis=1)
    log_g = np.log(np.asarray(_GAMMAS, np.float64))
    t = np.arange(rb, dtype=np.float64)
    dist = np.abs(t[:, None] - t[None, :])
    visible = (t[None, :] // CHUNK) <= (t[:, None] // CHUNK)
    dmat = np.where(visible[None], np.exp(dist[None] * log_g[:, None, None]), 0.0)
    qdec = np.exp((t[None, :] + 1.0) * log_g[:, None])
    kdec = np.exp((rb - 1.0 - t)[None, :] * log_g[:, None])
    bcast = lambda a: np.broadcast_to(a[:, :, None], (RET_HEADS, rb, HEAD_DIM))
    return tuple(jnp.asarray(a, jnp.float32) for a in (cos, sin, dmat, bcast(qdec), bcast(kdec)))


def kernel(x, p, g_mix, w_in, ret_norm_g, sg_norm_g, w_s, b_s, w_out, g_ffn, w_ff1, w_ff2,
           g_ple, w_ple_gate, w_ple, g_final):
    batch, seq, d = x.shape
    ts, tm, rb = SEQ_TILE, ROW_TILE, RET_BLOCK
    assert w_in.shape[0] == 1, "one layer: the final RMSNorm is fused into its channel mixer"
    assert d == D_MODEL and seq % ts == 0 and (batch * seq) % tm == 0
    cos, sin, dmat, qdec, kdec = _retention_tables(seq, rb)
    tiles_per_row = seq // ts
    assert tiles_per_row % 2 == 0, "a grid step mixes a pair of seq tiles of one batch row"
    pairs_per_row = tiles_per_row // 2
    n_tiles = batch * tiles_per_row

    def x_tile_spec(tile_of_step):
        def index(k):
            t = jnp.minimum(tile_of_step(k), n_tiles - 1)
            return (t // tiles_per_row, t % tiles_per_row, 0)
        return pl.BlockSpec((1, ts, d), index)

    tab_spec = pl.BlockSpec((2 * ts, HEAD_DIM), lambda k: (k % pairs_per_row, 0))
    pair_spec = pl.BlockSpec((1, 2 * ts, d), lambda k: (k // pairs_per_row, k % pairs_per_row, 0))
    z_slabs = pltpu.VMEM((IN_WIDTH // LANES, ts, LANES), jnp.float32)
    n_steps = n_tiles // 2

    def row_chunk(w):
        rows, cols = w.shape
        assert rows % (n_steps * BF16_SUBLANES) == 0, "a bf16 block needs whole packed sublane tiles"
        return pl.BlockSpec((rows // n_steps, cols), lambda k: (k, 0))

    ffn_weights = (w_ff1[0], w_ff2[0], w_ple_gate[0])
    x_keep = pltpu.VMEM((ts, d), jnp.float32)

    x, w_ff1_bf16, w_ff2_bf16, w_gate_bf16 = pl.pallas_call(
        functools.partial(_mixer_kernel, pairs_per_row=pairs_per_row),
        grid=(n_steps,),
        in_specs=[pl.BlockSpec((1, ts, d), lambda k: (0, 0, 0), pipeline_mode=pl.Buffered(1)),
                  x_tile_spec(lambda k: 2 * k + 1), x_tile_spec(lambda k: 2 * k + 2),
                  _const_spec((d, IN_WIDTH)),
                  tab_spec, tab_spec,
                  _const_spec((RET_HEADS, rb, rb)),
                  _const_spec((RET_HEADS, rb, HEAD_DIM)),
                  _const_spec((RET_HEADS, rb, HEAD_DIM)),
                  _const_spec((SG_GROUPS, SG_BLOCK, SG_BLOCK)),
                  _const_spec((SG_BLOCK, SG_GROUPS)),
                  _const_spec((d, d))] + [row_chunk(w) for w in ffn_weights]
        + [_const_spec((1, d)), _const_spec((1, RET_WIDTH)), _const_spec((1, SG_WIDTH))],
        out_specs=[pair_spec] + [row_chunk(w) for w in ffn_weights],
        out_shape=[jax.ShapeDtypeStruct((batch, seq, d), jnp.float32)]
        + [jax.ShapeDtypeStruct(w.shape, jnp.bfloat16) for w in ffn_weights],
        scratch_shapes=[pltpu.VMEM((RET_HEADS, HEAD_DIM, HEAD_DIM), jnp.float32),
                        z_slabs, z_slabs, x_keep, x_keep],
        compiler_params=pltpu.CompilerParams(
            dimension_semantics=("arbitrary",),
            vmem_limit_bytes=V7X_VMEM_LIMIT_BYTES),
        name="token_mixer",
    )(x, x, x, _bf16(w_in[0]), cos, sin, dmat, qdec, kdec, w_s[0], b_s[0].T, _bf16(w_out[0]),
      *ffn_weights, g_mix[0][None], ret_norm_g[0][None], sg_norm_g[0][None])

    row_spec = lambda w: pl.BlockSpec((tm, w), lambda i: (i, 0))
    x = pl.pallas_call(
        _ffn_kernel,
        grid=(batch * seq // tm,),
        in_specs=[row_spec(d), row_spec(PLE_DIM), _const_spec((1, d)),
                  _const_spec((d, D_FF)), _const_spec((D_FF, d)), _const_spec((1, d)),
                  _const_spec((d, d)), _const_spec((PLE_DIM, d)), _const_spec((1, d))],
        out_specs=row_spec(d),
        out_shape=jax.ShapeDtypeStruct((batch * seq, d), jnp.float32),
        compiler_params=pltpu.CompilerParams(
            dimension_semantics=("arbitrary",),
            vmem_limit_bytes=V7X_VMEM_LIMIT_BYTES),
        name="channel_mixer",
    )(x.reshape(batch * seq, d), p[0].reshape(batch * seq, PLE_DIM), g_ffn[0][None],
      w_ff1_bf16, w_ff2_bf16, g_ple[0][None], w_gate_bf16, _bf16(w_ple[0]), g_final[None])
    return x.reshape(batch, seq, d)
```
